```python
import math
import jax, jax.numpy as jnp
from jax import lax
import numpy as np


D_MODEL = 4096
BATCH = 2
SEQ = 4096
DEPTH = 4

GROUP_WIDTH = D_MODEL // 4
SGU_CHUNK = 128
SGU_HEADS = 8
SGU_HEAD_DIM = GROUP_WIDTH // SGU_HEADS
CONV_WIDTH = 31
CONV_GROUPS = 8
HGRN_EXPAND = 128
HGRN_HEADS = GROUP_WIDTH // HGRN_EXPAND
HGRN_HEAD_V = GROUP_WIDTH // HGRN_HEADS
HGRN_CHUNK = 64
ATTN_HEAD_DIM = 64
ATTN_Q_HEADS = GROUP_WIDTH // ATTN_HEAD_DIM
ATTN_KV_HEADS = 2
ATTN_GROUP = ATTN_Q_HEADS // ATTN_KV_HEADS
WINDOW = 128
ROPE_DIM = ATTN_HEAD_DIM // 4
ROPE_THETA = 500000.0
FFN_HIDDEN = ((8 * D_MODEL + 3 * 256 - 1) // (3 * 256)) * 256
SGU_COLS = 2 * GROUP_WIDTH
CONV_COLS = 2 * GROUP_WIDTH
HGRN_COLS = 4 * GROUP_WIDTH
ATTN_COLS = (ATTN_Q_HEADS + 2 * ATTN_KV_HEADS) * ATTN_HEAD_DIM
IN_COLS = SGU_COLS + CONV_COLS + HGRN_COLS + ATTN_COLS
DEEPNORM_ALPHA = (2 * DEPTH) ** 0.25
DEEPNORM_BETA = (8 * DEPTH) ** -0.25
LN_EPS = 1e-5

kernel_name = 'hybrid_sgu_conv_hgrn2_swa_deepnorm'


def layer_norm(x, g, b):
    xf = x.astype(jnp.float32)
    mu = jnp.mean(xf, axis=-1, keepdims=True)
    var = jnp.mean(jnp.square(xf - mu), axis=-1, keepdims=True)
    return ((xf - mu) * lax.rsqrt(var + LN_EPS) * g + b).astype(x.dtype)


def rope_partial(x, positions):
    half = ROPE_DIM // 2
    inv_freq = ROPE_THETA ** (-jnp.arange(0, ROPE_DIM, 2, dtype=jnp.float32) / ROPE_DIM)
    ang = positions.astype(jnp.float32)[..., None] * inv_freq
    cos, sin = jnp.cos(ang)[:, :, None, :], jnp.sin(ang)[:, :, None, :]
    xr = x[..., :ROPE_DIM].astype(jnp.float32)
    x1, x2 = xr[..., :half], xr[..., half:]
    rot = jnp.concatenate([x1 * cos - x2 * sin, x2 * cos + x1 * sin], axis=-1)
    return jnp.concatenate([rot.astype(x.dtype), x[..., ROPE_DIM:]], axis=-1)


def sgu_mixer(z, ln_g, ln_b, w_s, b_s):
    bsz, seq, _ = z.shape
    z = jax.nn.gelu(z, approximate=False)
    u, v = jnp.split(z, 2, axis=-1)
    v = layer_norm(v, ln_g, ln_b)
    n_chunks = seq // SGU_CHUNK
    v = v.reshape(bsz, n_chunks, SGU_CHUNK, SGU_HEADS, SGU_HEAD_DIM)
    causal = jnp.tril(jnp.ones((SGU_CHUNK, SGU_CHUNK), dtype=bool))
    w = jnp.where(causal[None], w_s, 0)
    mixed = jnp.einsum('hts,bnshc->bnthc', w, v) + b_s.T[None, None, :, :, None]
    return u * mixed.reshape(bsz, seq, GROUP_WIDTH)


def conv_module(z, dw_w, dw_b, norm_g, norm_b, pw_w, pw_b):
    bsz, seq, _ = z.shape
    h = jax.nn.glu(z, axis=-1)
    h = jnp.pad(h, ((0, 0), (CONV_WIDTH - 1, 0), (0, 0)))
    h = lax.conv_general_dilated(h, dw_w[:, None, :], window_strides=(1,), padding='VALID',
                                 dimension_numbers=('NWC', 'WIO', 'NWC'),
                                 feature_group_count=GROUP_WIDTH) + dw_b
    hg = h.reshape(bsz, seq, CONV_GROUPS, GROUP_WIDTH // CONV_GROUPS).astype(jnp.float32)
    mu = jnp.mean(hg, axis=-1, keepdims=True)
    var = jnp.mean(jnp.square(hg - mu), axis=-1, keepdims=True)
    hg = ((hg - mu) * lax.rsqrt(var + LN_EPS)).reshape(bsz, seq, GROUP_WIDTH)
    h = jax.nn.silu(hg * norm_g + norm_b).astype(z.dtype)
    return h @ pw_w + pw_b


def hgrn2_mixer(zq, zf, zi, zg, lower_bound, norm_g):
    bsz, seq, _ = zq.shape
    n_chunks = seq // HGRN_CHUNK
    f32 = jnp.float32
    q = jax.nn.silu(zq.astype(f32))
    f = lower_bound + (1.0 - lower_bound) * jax.nn.sigmoid(zf.astype(f32))
    log_f = jnp.log(f)
    k = 1.0 - f
    v = zi.astype(f32)

    def to_chunks(a):
        a = a.reshape(bsz, n_chunks, HGRN_CHUNK, HGRN_HEADS, -1)
        return a.transpose(1, 0, 3, 2, 4)

    causal = jnp.tril(jnp.ones((HGRN_CHUNK, HGRN_CHUNK), dtype=bool))[:, :, None]

    def chunk_step(state, inp):
        qc, kc, vc, lfc = inp
        b = jnp.cumsum(lfc, axis=2)
        o_inter = jnp.einsum('bhtd,bhdv->bhtv', qc * jnp.exp(b), state)
        diff = b[:, :, :, None, :] - b[:, :, None, :, :]
        decay = jnp.exp(jnp.where(causal, diff, -jnp.inf))
        scores = jnp.einsum('bhtd,bhsd,bhtsd->bhts', qc, kc, decay)
        o_intra = jnp.einsum('bhts,bhsv->bhtv', scores, vc)
        b_end = b[:, :, -1:, :]
        new_state = (jnp.exp(b_end[:, :, 0, :])[..., None] * state
                     + jnp.einsum('bhsd,bhsv->bhdv', kc * jnp.exp(b_end - b), vc))
        return new_state, o_inter + o_intra

    state0 = jnp.zeros((bsz, HGRN_HEADS, HGRN_EXPAND, HGRN_HEAD_V), f32)
    _, o = lax.scan(chunk_step, state0,
                    (to_chunks(q), to_chunks(k), to_chunks(v), to_chunks(log_f)))
    o = o.transpose(1, 0, 3, 2, 4).reshape(bsz, seq, HGRN_HEADS, HGRN_HEAD_V)
    o = o * lax.rsqrt(jnp.mean(jnp.square(o), axis=-1, keepdims=True) + LN_EPS)
    o = o.reshape(bsz, seq, GROUP_WIDTH) * norm_g * jax.nn.sigmoid(zg.astype(f32))
    return o.astype(zq.dtype)


def swa_attention(zq, zk, zv, positions, sinks):
    bsz, seq, _ = zq.shape
    nb = seq // WINDOW
    q = rope_partial(zq.reshape(bsz, seq, ATTN_Q_HEADS, ATTN_HEAD_DIM), positions)
    k = rope_partial(zk.reshape(bsz, seq, ATTN_KV_HEADS, ATTN_HEAD_DIM), positions)
    v = zv.reshape(bsz, seq, ATTN_KV_HEADS, ATTN_HEAD_DIM)
    q = q.reshape(bsz, nb, WINDOW, ATTN_KV_HEADS, ATTN_GROUP, ATTN_HEAD_DIM)

    def band(a):
        a = jnp.pad(a, ((0, 0), (WINDOW, 0), (0, 0), (0, 0)))
        a = a.reshape(bsz, nb + 1, WINDOW, ATTN_KV_HEADS, ATTN_HEAD_DIM)
        return jnp.concatenate([a[:, :-1], a[:, 1:]], axis=2)

    kb, vb = band(k), band(v)
    s = jnp.einsum('bnqkgd,bnskd->bnkgqs', q, kb).astype(jnp.float32) * (ATTN_HEAD_DIM ** -0.5)
    i = jnp.arange(WINDOW)[:, None]
    j = jnp.arange(2 * WINDOW)[None, :]
    blk = jnp.arange(nb)[:, None, None]
    valid = (j > i) & (j <= i + WINDOW) & (blk * WINDOW - WINDOW + j >= 0)
    s = jnp.where(valid[None, :, None, None], s, -jnp.inf)
    sink = sinks.astype(jnp.float32).reshape(ATTN_KV_HEADS, ATTN_GROUP)[None, None, :, :, None, None]
    m = jnp.maximum(jnp.max(s, axis=-1, keepdims=True), sink)
    p = jnp.exp(s - m)
    p = p / (jnp.sum(p, axis=-1, keepdims=True) + jnp.exp(sink - m))
    o = jnp.einsum('bnkgqs,bnskd->bnqkgd', p.astype(v.dtype), vb)
    return o.reshape(bsz, seq, ATTN_Q_HEADS * ATTN_HEAD_DIM)


def setup_inputs(seed: int = 0) -> dict:
    key = jax.random.key(seed)
    ks = jax.random.split(key, 26)
    f32 = jnp.float32
    L, D, GW, F = DEPTH, D_MODEL, GROUP_WIDTH, FFN_HIDDEN

    def nrm(k, shape, scale):
        return jax.random.normal(k, shape, f32) * scale

    def gain(k, shape):
        return 1.0 + 0.02 * jax.random.normal(k, shape, f32)

    positions = (jax.random.randint(ks[1], (BATCH, 1), 0, 2048, dtype=jnp.int32)
                 + jnp.arange(SEQ, dtype=jnp.int32)[None, :])
    return {
        'x': jax.random.normal(ks[0], (BATCH, SEQ, D), f32),
        'positions': positions,
        'emb_ln_g': gain(ks[2], (D,)),
        'emb_ln_b': nrm(ks[3], (D,), 0.02),
        'w_in': nrm(ks[4], (L, D, IN_COLS), D ** -0.5),
        'sgu_ln_g': gain(ks[5], (L, GW)),
        'sgu_ln_b': nrm(ks[6], (L, GW), 0.02),
        'sgu_w': nrm(ks[7], (L, SGU_HEADS, SGU_CHUNK, SGU_CHUNK), SGU_CHUNK ** -0.5),
        'sgu_b': gain(ks[8], (L, SGU_HEADS, SGU_CHUNK)),
        'conv_dw_w': nrm(ks[9], (L, CONV_WIDTH, GW), CONV_WIDTH ** -0.5),
        'conv_dw_b': nrm(ks[10], (L, GW), 0.02),
        'conv_norm_g': gain(ks[11], (L, GW)),
        'conv_norm_b': nrm(ks[12], (L, GW), 0.02),
        'conv_pw_w': nrm(ks[13], (L, GW, GW), GW ** -0.5),
        'conv_pw_b': nrm(ks[14], (L, GW), 0.02),
        'hgrn_lower_bound': nrm(ks[15], (L, GW), 0.5),
        'hgrn_norm_g': gain(ks[16], (L, GW)),
        'attn_sinks': nrm(ks[17], (L, ATTN_Q_HEADS), 0.5),
        'w_o': nrm(ks[18], (L, D, D), DEEPNORM_BETA * D ** -0.5),
        'ln1_g': gain(ks[19], (L, D)),
        'ln1_b': nrm(ks[20], (L, D), 0.02),
        'w_gate': nrm(ks[21], (L, D, F), D ** -0.5),
        'w_up': nrm(ks[22], (L, D, F), D ** -0.5),
        'w_down': nrm(ks[23], (L, F, D), DEEPNORM_BETA * F ** -0.5),
        'ln2_g': gain(ks[24], (L, D)),
        'ln2_b': nrm(ks[25], (L, D), 0.02),
    }


def reference(x, positions, emb_ln_g, emb_ln_b, w_in, sgu_ln_g, sgu_ln_b, sgu_w, sgu_b,
              conv_dw_w, conv_dw_b, conv_norm_g, conv_norm_b, conv_pw_w, conv_pw_b,
              hgrn_lower_bound, hgrn_norm_g, attn_sinks, w_o, ln1_g, ln1_b,
              w_gate, w_up, w_down, ln2_g, ln2_b):
    p_lb = jax.nn.softmax(hgrn_lower_bound.astype(jnp.float32), axis=0)
    lower_bounds = jnp.cumsum(p_lb, axis=0) - p_lb[0]
    split_at = [SGU_COLS, SGU_COLS + CONV_COLS, SGU_COLS + CONV_COLS + HGRN_COLS]
    q_cols = ATTN_Q_HEADS * ATTN_HEAD_DIM
    kv_cols = ATTN_KV_HEADS * ATTN_HEAD_DIM

    x = layer_norm(x, emb_ln_g, emb_ln_b)
    for l in range(DEPTH):
        h = x @ w_in[l]
        h_sgu, h_conv, h_hgrn, h_attn = jnp.split(h, split_at, axis=-1)
        y_a = sgu_mixer(h_sgu, sgu_ln_g[l], sgu_ln_b[l], sgu_w[l], sgu_b[l])
        y_b = conv_module(h_conv, conv_dw_w[l], conv_dw_b[l], conv_norm_g[l], conv_norm_b[l],
                          conv_pw_w[l], conv_pw_b[l])
        zq, zf, zi, zg = jnp.split(h_hgrn, 4, axis=-1)
        y_c = hgrn2_mixer(zq, zf, zi, zg, lower_bounds[l], hgrn_norm_g[l])
        aq, ak, av = jnp.split(h_attn, [q_cols, q_cols + kv_cols], axis=-1)
        y_d = swa_attention(aq, ak, av, positions, attn_sinks[l])
        mix = jnp.concatenate([y_a, y_b, y_c, y_d], axis=-1) @ w_o[l]
        x = layer_norm(DEEPNORM_ALPHA * x + mix, ln1_g[l], ln1_b[l])
        ffn = (jax.nn.silu(x @ w_gate[l]) * (x @ w_up[l])) @ w_down[l]
        x = layer_norm(DEEPNORM_ALPHA * x + ffn, ln2_g[l], ln2_b[l])
    return x
```

```python
import functools
import math

import numpy as np
import jax
import jax.numpy as jnp
from jax import lax
from jax.experimental import pallas as pl
from jax.experimental.pallas import tpu as pltpu

F32 = jnp.float32
BF16 = jnp.bfloat16

D_MODEL = 4096
DEPTH = 4
GW = D_MODEL // 4
LANES = 128
SGU_CHUNK = 128
SGU_HEADS = 8
CONV_WIDTH = 31
CONV_HALO = 32
HGRN_HEADS = 8
HGRN_CHUNK = 128
HGRN_SUB = 8
ATTN_HEAD_DIM = 64
ATTN_Q_HEADS = 16
ATTN_KV_HEADS = 2
WINDOW = 128
ROPE_DIM = 16
ROPE_THETA = 500000.0
FFN_HIDDEN = 11008
FFN_PAD = 11264
IN_COLS = 2 * GW + 2 * GW + 4 * GW + (ATTN_Q_HEADS + 2 * ATTN_KV_HEADS) * ATTN_HEAD_DIM
ALPHA = (2 * DEPTH) ** 0.25
LN_EPS = 1e-5
VMEM_LIMIT = 56 * 1024 * 1024


def _params(sem):
    return pltpu.CompilerParams(dimension_semantics=sem, vmem_limit_bytes=VMEM_LIMIT)


def _sigmoid(x):
    return 1.0 / (1.0 + jnp.exp(-x))


def _gelu(x):
    return 0.5 * x * (1.0 + lax.erf(x * (1.0 / math.sqrt(2.0))))


def _layer_norm(x, g, b):
    mu = jnp.mean(x, axis=-1, keepdims=True)
    xc = x - mu
    var = jnp.mean(xc * xc, axis=-1, keepdims=True)
    return xc * lax.rsqrt(var + LN_EPS) * g + b


def _dot(a, b):
    return jnp.dot(a, b, preferred_element_type=F32)


def _dot_nt(a, b):
    return lax.dot_general(a, b, (((1,), (1,)), ((), ())), preferred_element_type=F32)


def _dot_tn(a, b):
    return lax.dot_general(a, b, (((0,), (0,)), ((), ())), preferred_element_type=F32)


def _emb_ln_kernel(x_ref, g_ref, b_ref, o_ref, ob_ref):
    y = _layer_norm(x_ref[...], g_ref[...], b_ref[...])
    o_ref[...] = y
    ob_ref[...] = y.astype(BF16)


def _emb_ln(x, g, b, tm=256):
    n, d = x.shape
    return pl.pallas_call(
        _emb_ln_kernel,
        grid=(n // tm,),
        in_specs=[pl.BlockSpec((tm, d), lambda i: (i, 0)),
                  pl.BlockSpec((1, d), lambda i: (0, 0)),
                  pl.BlockSpec((1, d), lambda i: (0, 0))],
        out_specs=[pl.BlockSpec((tm, d), lambda i: (i, 0)),
                   pl.BlockSpec((tm, d), lambda i: (i, 0))],
        out_shape=[jax.ShapeDtypeStruct((n, d), F32), jax.ShapeDtypeStruct((n, d), BF16)],
        compiler_params=_params(("arbitrary",)),
        name="emb_ln",
    )(x, g.reshape(1, d), b.reshape(1, d))


def _in_proj_kernel(a_ref, w_ref, o_ref):
    o_ref[...] = _dot(a_ref[...], w_ref[...])


def _in_proj(xb, w, tm=1024, tn=256):
    n, d = xb.shape
    cols = w.shape[1]
    tm = min(tm, n)
    return pl.pallas_call(
        _in_proj_kernel,
        grid=(n // tm, cols // tn),
        in_specs=[pl.BlockSpec((tm, d), lambda i, j: (i, 0)),
                  pl.BlockSpec((d, tn), lambda i, j: (0, j))],
        out_specs=pl.BlockSpec((tm, tn), lambda i, j: (i, j)),
        out_shape=jax.ShapeDtypeStruct((n, cols), F32),
        compiler_params=_params(("arbitrary", "arbitrary")),
        name="in_proj",
    )(xb, w)


def _sgu_kernel(u_ref, v_ref, g_ref, b_ref, w_ref, bs_ref, o_ref, *, tt):
    row = lax.broadcasted_iota(jnp.int32, (SGU_CHUNK, SGU_CHUNK), 0)
    col = lax.broadcasted_iota(jnp.int32, (SGU_CHUNK, SGU_CHUNK), 1)
    causal = row >= col
    for c in range(tt // SGU_CHUNK):
        rows = slice(c * SGU_CHUNK, (c + 1) * SGU_CHUNK)
        u = _gelu(u_ref[rows, :])
        v = _layer_norm(_gelu(v_ref[rows, :]), g_ref[...], b_ref[...]).astype(BF16)
        for hd in range(SGU_HEADS):
            cols = slice(hd * LANES, (hd + 1) * LANES)
            w = jnp.where(causal, w_ref[hd], 0.0).astype(BF16)
            mixed = _dot(w, v[:, cols]) + bs_ref[:, hd:hd + 1]
            o_ref[rows, cols] = (u[:, cols] * mixed).astype(o_ref.dtype)


def _sgu(h, ln_g, ln_b, w_s, b_s, tt=256):
    n = h.shape[0]
    return pl.pallas_call(
        functools.partial(_sgu_kernel, tt=tt),
        grid=(n // tt,),
        in_specs=[pl.BlockSpec((tt, GW), lambda i: (i, 0)),
                  pl.BlockSpec((tt, GW), lambda i: (i, 1)),
                  pl.BlockSpec((1, GW), lambda i: (0, 0)),
                  pl.BlockSpec((1, GW), lambda i: (0, 0)),
                  pl.BlockSpec((SGU_HEADS, SGU_CHUNK, SGU_CHUNK), lambda i: (0, 0, 0)),
                  pl.BlockSpec((SGU_CHUNK, SGU_HEADS), lambda i: (0, 0))],
        out_specs=pl.BlockSpec((tt, GW), lambda i: (i, 0)),
        out_shape=jax.ShapeDtypeStruct((n, GW), BF16),
        compiler_params=_params(("arbitrary",)),
        name="sgu_mixer",
    )(h, h, ln_g.reshape(1, GW), ln_b.reshape(1, GW), w_s, b_s.T)


def _conv_kernel(a_ref, gate_ref, dww_ref, dwb_ref, ng_ref, nb_ref, pww_ref, pwb_ref, o_ref,
                 hbuf, act, *, tt):
    @pl.when(pl.program_id(1) == 0)
    def _():
        hbuf[0:CONV_HALO, :] = jnp.zeros((CONV_HALO, GW), F32)

    hbuf[CONV_HALO:CONV_HALO + tt, :] = a_ref[...] * _sigmoid(gate_ref[...])
    first = CONV_HALO - (CONV_WIDTH - 1)
    for c in range(GW // LANES):
        cols = slice(c * LANES, (c + 1) * LANES)
        acc = jnp.broadcast_to(dwb_ref[:, cols], (tt, LANES))
        for k in range(CONV_WIDTH):
            acc = acc + dww_ref[k:k + 1, cols] * hbuf[first + k:first + k + tt, cols]
        mu = jnp.mean(acc, axis=-1, keepdims=True)
        xc = acc - mu
        var = jnp.mean(xc * xc, axis=-1, keepdims=True)
        hn = xc * lax.rsqrt(var + LN_EPS) * ng_ref[:, cols] + nb_ref[:, cols]
        act[:, cols] = (hn * _sigmoid(hn)).astype(BF16)
    o_ref[...] = (_dot(act[...], pww_ref[...]) + pwb_ref[...]).astype(o_ref.dtype)
    hbuf[0:CONV_HALO, :] = hbuf[tt:tt + CONV_HALO, :]


def _conv(h, bsz, dw_w, dw_b, norm_g, norm_b, pw_w, pw_b, tt=256):
    n = h.shape[0]
    nt = n // bsz // tt
    row = lambda b, t: b * nt + t
    vec = lambda: pl.BlockSpec((1, GW), lambda b, t: (0, 0))
    return pl.pallas_call(
        functools.partial(_conv_kernel, tt=tt),
        grid=(bsz, nt),
        in_specs=[pl.BlockSpec((tt, GW), lambda b, t: (row(b, t), 2)),
                  pl.BlockSpec((tt, GW), lambda b, t: (row(b, t), 3)),
                  pl.BlockSpec((CONV_WIDTH, GW), lambda b, t: (0, 0)),
                  vec(), vec(), vec(),
                  pl.BlockSpec((GW, GW), lambda b, t: (0, 0)),
                  vec()],
        out_specs=pl.BlockSpec((tt, GW), lambda b, t: (row(b, t), 0)),
        out_shape=jax.ShapeDtypeStruct((n, GW), BF16),
        scratch_shapes=[pltpu.VMEM((CONV_HALO + tt, GW), F32), pltpu.VMEM((tt, GW), BF16)],
        compiler_params=_params(("arbitrary", "arbitrary")),
        name="conv_mixer",
    )(h, h, dw_w, dw_b.reshape(1, GW), norm_g.reshape(1, GW), norm_b.reshape(1, GW),
      pw_w, pw_b.reshape(1, GW))


def _hgrn_constants():
    t = np.arange(HGRN_CHUNK)[:, None]
    j = np.arange(HGRN_CHUNK)[None, :]
    tril = (t >= j).astype(np.float32)
    level = np.where(t >= j, t ^ j, 1 << 20).astype(np.int32)
    shift = np.where((t >= j) & ((t // HGRN_SUB) == (j // HGRN_SUB)), t - j, -1).astype(np.int32)
    return jnp.asarray(tril, BF16), jnp.asarray(level), jnp.asarray(shift)


def _hgrn_chunk(zq, zf, zi, zg, lb, ng, state, tril, level, shift, ones, bbuf, kbuf, pbuf):
    c = HGRN_CHUNK
    q = zq * _sigmoid(zq)
    f = lb + (1.0 - lb) * _sigmoid(zf)
    lf = jnp.log(f)
    k = 1.0 - f
    v = zi.astype(BF16)
    hi = lf.astype(BF16)
    r1 = lf - hi.astype(F32)
    mid = r1.astype(BF16)
    lo = (r1 - mid.astype(F32)).astype(BF16)
    b = _dot(tril, hi) + _dot(tril, mid) + _dot(tril, lo)
    bbuf[HGRN_SUB:HGRN_SUB + c, :] = b
    kbuf[HGRN_SUB:HGRN_SUB + c, :] = k

    o = _dot_nt((q * jnp.exp(b)).astype(BF16), state.astype(BF16))

    parts = []
    for m in (64, 32, 16, 8):
        refs = [jnp.broadcast_to(bbuf[HGRN_SUB + g * 2 * m + m - 1:HGRN_SUB + g * 2 * m + m, :], (2 * m, LANES))
                for g in range(c // (2 * m))]
        ref = refs[0] if len(refs) == 1 else jnp.concatenate(refs, axis=0)
        e = jnp.exp(-jnp.abs(b - ref))
        parts.append(_dot_nt((q * e).astype(BF16), (k * e).astype(BF16)))

    for s in range(HGRN_SUB):
        bs = bbuf[HGRN_SUB - s:HGRN_SUB - s + c, :]
        ks = kbuf[HGRN_SUB - s:HGRN_SUB - s + c, :]
        p = q * ks * jnp.exp(jnp.minimum(b - bs, 0.0))
        pbuf[s * c:(s + 1) * c, :] = p.astype(BF16)
    near = _dot(pbuf[...], ones)

    scores = jnp.zeros((c, c), F32)
    for m, part in zip((64, 32, 16, 8), parts):
        scores = jnp.where((level >= m) & (level < 2 * m), part, scores)
    for s in range(HGRN_SUB):
        scores = jnp.where(shift == s, near[s * c:(s + 1) * c, :], scores)
    o = o + _dot(scores.astype(BF16), v)

    b_end = b[c - 1:c, :]
    new_state = jnp.exp(b_end) * state + _dot_tn(v, (k * jnp.exp(b_end - b)).astype(BF16))

    o = o * lax.rsqrt(jnp.mean(o * o, axis=-1, keepdims=True) + LN_EPS)
    return o * ng * _sigmoid(zg), new_state


def _hgrn_kernel(zq_ref, zf_ref, zi_ref, zg_ref, lb_ref, ng_ref, tril_ref, level_ref, shift_ref, o_ref,
                 state, bbuf, kbuf, pbuf, *, tt):
    @pl.when(pl.program_id(2) == 0)
    def _():
        state[...] = jnp.zeros_like(state)
        bbuf[0:HGRN_SUB, :] = jnp.zeros((HGRN_SUB, LANES), F32)
        kbuf[0:HGRN_SUB, :] = jnp.zeros((HGRN_SUB, LANES), F32)

    ones = jnp.ones((LANES, LANES), BF16)
    for ci in range(tt // HGRN_CHUNK):
        rows = slice(ci * HGRN_CHUNK, (ci + 1) * HGRN_CHUNK)
        o, st = _hgrn_chunk(zq_ref[rows, :], zf_ref[rows, :], zi_ref[rows, :], zg_ref[rows, :],
                            lb_ref[0], ng_ref[0], state[...], tril_ref[...], level_ref[...], shift_ref[...],
                            ones, bbuf, kbuf, pbuf)
        state[...] = st
        o_ref[rows, :] = o.astype(o_ref.dtype)


def _hgrn(h, bsz, lower_bound, norm_g, tt=512):
    n = h.shape[0]
    nt = n // bsz // tt
    tril, level, shift = _hgrn_constants()
    base = 4 * GW // LANES
    heads = GW // LANES

    def zspec(part):
        return pl.BlockSpec((tt, LANES), lambda b, hd, t: (b * nt + t, base + part * heads + hd))

    const = lambda: pl.BlockSpec((HGRN_CHUNK, HGRN_CHUNK), lambda b, hd, t: (0, 0))
    vec = lambda: pl.BlockSpec((1, 1, LANES), lambda b, hd, t: (hd, 0, 0))
    return pl.pallas_call(
        functools.partial(_hgrn_kernel, tt=tt),
        grid=(bsz, heads, nt),
        in_specs=[zspec(0), zspec(1), zspec(2), zspec(3), vec(), vec(), const(), const(), const()],
        out_specs=pl.BlockSpec((tt, LANES), lambda b, hd, t: (b * nt + t, hd)),
        out_shape=jax.ShapeDtypeStruct((n, GW), BF16),
        scratch_shapes=[pltpu.VMEM((LANES, LANES), F32),
                        pltpu.VMEM((HGRN_SUB + HGRN_CHUNK, LANES), F32),
                        pltpu.VMEM((HGRN_SUB + HGRN_CHUNK, LANES), F32),
                        pltpu.VMEM((HGRN_SUB * HGRN_CHUNK, LANES), BF16)],
        compiler_params=_params(("arbitrary", "arbitrary", "arbitrary")),
        name="hgrn_mixer",
    )(h, h, h, h, lower_bound.reshape(heads, 1, LANES), norm_g.reshape(heads, 1, LANES), tril, level, shift)


def _rope_constants():
    lane = np.arange(LANES) % ATTN_HEAD_DIM
    half = ROPE_DIM // 2
    inv_freq = ROPE_THETA ** (-jnp.arange(0, ROPE_DIM, 2, dtype=F32) / ROPE_DIM)
    freq = jnp.where(lane < ROPE_DIM, inv_freq[lane % half], 0.0).astype(F32)
    rot = (lane < ROPE_DIM).astype(np.float32)
    up = ((lane >= half) & (lane < ROPE_DIM)).astype(np.float32)
    dn = (lane < half).astype(np.float32)
    return jnp.stack([freq, jnp.asarray(rot), jnp.asarray(up), jnp.asarray(dn)])[:, None, :]


def _swa_kernel(sink_ref, pos_ref, q_ref, k_ref, v_ref, rc_ref, o_ref, kprev, vprev):
    t = pl.program_id(1)
    w = WINDOW

    @pl.when(t == 0)
    def _():
        kprev[...] = jnp.zeros_like(kprev)
        vprev[...] = jnp.zeros_like(vprev)

    ang = pos_ref[...].astype(F32) * rc_ref[0]
    cos, sin = jnp.cos(ang), jnp.sin(ang)
    c_mul = cos * rc_ref[1] + (1.0 - rc_ref[1])
    s_up = sin * rc_ref[2]
    s_dn = -sin * rc_ref[3]
    half = ROPE_DIM // 2

    def rope(x):
        return x * c_mul + pltpu.roll(x, half, 1) * s_up + pltpu.roll(x, LANES - half, 1) * s_dn

    lane = lax.broadcasted_iota(jnp.int32, (w, LANES), 1)
    low = lane < ATTN_HEAD_DIM
    k_cur = rope(k_ref[...])
    v_cur = v_ref[...]
    k_sw = pltpu.roll(k_cur, ATTN_HEAD_DIM, 1)
    v_sw = pltpu.roll(v_cur, ATTN_HEAD_DIM, 1)
    k_dup = [jnp.where(low, k_cur, k_sw).astype(BF16), jnp.where(low, k_sw, k_cur).astype(BF16)]
    v_dup = [jnp.where(low, v_cur, v_sw).astype(BF16), jnp.where(low, v_sw, v_cur).astype(BF16)]

    qi = lax.broadcasted_iota(jnp.int32, (w, w), 0)
    kj = lax.broadcasted_iota(jnp.int32, (w, w), 1)
    valid_prev = kj > qi + jnp.where(t > 0, 0, w)
    valid_cur = kj <= qi
    scale = ATTN_HEAD_DIM ** -0.5
    group = ATTN_Q_HEADS // ATTN_KV_HEADS

    for pair in range(ATTN_Q_HEADS // 2):
        cols = slice(pair * LANES, (pair + 1) * LANES)
        kv = (2 * pair) // group
        qt = rope(q_ref[:, cols])
        outs = []
        for side in range(2):
            head = 2 * pair + side
            qh = jnp.where(low if side == 0 else ~low, qt, 0.0).astype(BF16)
            s_prev = jnp.where(valid_prev, _dot_nt(qh, kprev[kv]) * scale, -jnp.inf)
            s_cur = jnp.where(valid_cur, _dot_nt(qh, k_dup[kv]) * scale, -jnp.inf)
            sink = sink_ref[head]
            m = jnp.maximum(jnp.maximum(jnp.max(s_prev, axis=-1, keepdims=True),
                                        jnp.max(s_cur, axis=-1, keepdims=True)), sink)
            p_prev = jnp.exp(s_prev - m)
            p_cur = jnp.exp(s_cur - m)
            denom = (jnp.sum(p_prev, axis=-1, keepdims=True) + jnp.sum(p_cur, axis=-1, keepdims=True)
                     + jnp.exp(sink - m))
            acc = _dot(p_prev.astype(BF16), vprev[kv]) + _dot(p_cur.astype(BF16), v_dup[kv])
            outs.append(acc / denom)
        o_ref[:, cols] = jnp.where(low, outs[0], outs[1]).astype(o_ref.dtype)

    for kv in range(ATTN_KV_HEADS):
        kprev[kv] = k_dup[kv]
        vprev[kv] = v_dup[kv]


def _swa(h, bsz, positions, sinks):
    n = h.shape[0]
    nt = n // bsz // WINDOW
    row = lambda b, t: b * nt + t
    qblk = 8
    kblk = (9 * GW) // LANES
    return pl.pallas_call(
        _swa_kernel,
        grid=(bsz, nt),
        in_specs=[pl.BlockSpec(memory_space=pltpu.SMEM),
                  pl.BlockSpec((WINDOW, 1), lambda b, t: (row(b, t), 0)),
                  pl.BlockSpec((WINDOW, GW), lambda b, t: (row(b, t), qblk)),
                  pl.BlockSpec((WINDOW, LANES), lambda b, t: (row(b, t), kblk)),
                  pl.BlockSpec((WINDOW, LANES), lambda b, t: (row(b, t), kblk + 1)),
                  pl.BlockSpec((4, 1, LANES), lambda b, t: (0, 0, 0))],
        out_specs=pl.BlockSpec((WINDOW, GW), lambda b, t: (row(b, t), 0)),
        out_shape=jax.ShapeDtypeStruct((n, GW), BF16),
        scratch_shapes=[pltpu.VMEM((ATTN_KV_HEADS, WINDOW, LANES), BF16),
                        pltpu.VMEM((ATTN_KV_HEADS, WINDOW, LANES), BF16)],
        compiler_params=_params(("arbitrary", "arbitrary")),
        name="swa_mixer",
    )(sinks, positions.reshape(n, 1), h, h, h, _rope_constants())


def _out_proj_kernel(ya_ref, yb_ref, yc_ref, yd_ref, w_ref, x_ref, g_ref, b_ref, o_ref, ob_ref):
    k = pl.program_id(1)

    @pl.when(k == 0)
    def _():
        o_ref[...] = ALPHA * x_ref[...] + _dot(ya_ref[...], w_ref[...])

    @pl.when(k == 1)
    def _():
        o_ref[...] += _dot(yb_ref[...], w_ref[...])

    @pl.when(k == 2)
    def _():
        o_ref[...] += _dot(yc_ref[...], w_ref[...])

    @pl.when(k == 3)
    def _():
        y = _layer_norm(o_ref[...] + _dot(yd_ref[...], w_ref[...]), g_ref[...], b_ref[...])
        o_ref[...] = y
        ob_ref[...] = y.astype(BF16)


def _out_proj(ys, w, x, g, b, tm=256):
    n, d = x.shape
    yspec = lambda: pl.BlockSpec((tm, GW), lambda i, k: (i, 0))
    row = lambda: pl.BlockSpec((tm, d), lambda i, k: (i, 0))
    vec = lambda: pl.BlockSpec((1, d), lambda i, k: (0, 0))
    return pl.pallas_call(
        _out_proj_kernel,
        grid=(n // tm, 4),
        in_specs=[yspec(), yspec(), yspec(), yspec(),
                  pl.BlockSpec((GW, d), lambda i, k: (k, 0)),
                  row(), vec(), vec()],
        out_specs=[row(), row()],
        out_shape=[jax.ShapeDtypeStruct((n, d), F32), jax.ShapeDtypeStruct((n, d), BF16)],
        compiler_params=_params(("arbitrary", "arbitrary")),
        name="out_proj_ln",
    )(*ys, w, x, g.reshape(1, d), b.reshape(1, d))


def _ffn_up_kernel(x_ref, wg_ref, wu_ref, o_ref):
    x = x_ref[...]
    gate = _dot(x, wg_ref[...])
    up = _dot(x, wu_ref[...])
    o_ref[...] = (gate * _sigmoid(gate) * up).astype(o_ref.dtype)


def _ffn_up(xb, wg, wu, tm=1024, tn=512):
    n, d = xb.shape
    f = wg.shape[1]
    tm = min(tm, n)
    return pl.pallas_call(
        _ffn_up_kernel,
        grid=(n // tm, f // tn),
        in_specs=[pl.BlockSpec((tm, d), lambda i, j: (i, 0)),
                  pl.BlockSpec((d, tn), lambda i, j: (0, j)),
                  pl.BlockSpec((d, tn), lambda i, j: (0, j))],
        out_specs=pl.BlockSpec((tm, tn), lambda i, j: (i, j)),
        out_shape=jax.ShapeDtypeStruct((n, f), BF16),
        compiler_params=_params(("arbitrary", "arbitrary")),
        name="ffn_up",
    )(xb, wg, wu)


def _ffn_down_kernel(h_ref, w_ref, x_ref, g_ref, b_ref, o_ref, ob_ref):
    k = pl.program_id(1)

    @pl.when(k == 0)
    def _():
        o_ref[...] = ALPHA * x_ref[...] + _dot(h_ref[...], w_ref[...])

    @pl.when((k > 0) & (k < pl.num_programs(1) - 1))
    def _():
        o_ref[...] += _dot(h_ref[...], w_ref[...])

    @pl.when(k == pl.num_programs(1) - 1)
    def _():
        y = _layer_norm(o_ref[...] + _dot(h_ref[...], w_ref[...]), g_ref[...], b_ref[...])
        o_ref[...] = y
        ob_ref[...] = y.astype(BF16)


def _ffn_down(hid, w, x, g, b, tm=256, tk=512):
    n, d = x.shape
    f = hid.shape[1]
    row = lambda: pl.BlockSpec((tm, d), lambda i, k: (i, 0))
    vec = lambda: pl.BlockSpec((1, d), lambda i, k: (0, 0))
    return pl.pallas_call(
        _ffn_down_kernel,
        grid=(n // tm, f // tk),
        in_specs=[pl.BlockSpec((tm, tk), lambda i, k: (i, k)),
                  pl.BlockSpec((tk, d), lambda i, k: (k, 0)),
                  row(), vec(), vec()],
        out_specs=[row(), row()],
        out_shape=[jax.ShapeDtypeStruct((n, d), F32), jax.ShapeDtypeStruct((n, d), BF16)],
        compiler_params=_params(("arbitrary", "arbitrary")),
        name="ffn_down_ln",
    )(hid, w, x, g.reshape(1, d), b.reshape(1, d))


def kernel(x, positions, emb_ln_g, emb_ln_b, w_in, sgu_ln_g, sgu_ln_b, sgu_w, sgu_b, conv_dw_w, conv_dw_b, conv_norm_g, conv_norm_b, conv_pw_w, conv_pw_b, hgrn_lower_bound, hgrn_norm_g, attn_sinks, w_o, ln1_g, ln1_b, w_gate, w_up, w_down, ln2_g, ln2_b):
    bsz, seq, d = x.shape
    n = bsz * seq
    depth = w_in.shape[0]
    p_lb = jax.nn.softmax(hgrn_lower_bound.astype(F32), axis=0)
    lower_bounds = jnp.cumsum(p_lb, axis=0) - p_lb[0]
    pad = FFN_PAD - w_gate.shape[2]

    xf, xb = _emb_ln(x.reshape(n, d), emb_ln_g, emb_ln_b)
    for l in range(depth):
        h = _in_proj(xb, w_in[l].astype(BF16))
        y_a = _sgu(h, sgu_ln_g[l], sgu_ln_b[l], sgu_w[l], sgu_b[l])
        y_b = _conv(h, bsz, conv_dw_w[l], conv_dw_b[l], conv_norm_g[l], conv_norm_b[l],
                    conv_pw_w[l].astype(BF16), conv_pw_b[l])
        y_c = _hgrn(h, bsz, lower_bounds[l], hgrn_norm_g[l])
        y_d = _swa(h, bsz, positions, attn_sinks[l])
        xf, xb = _out_proj((y_a, y_b, y_c, y_d), w_o[l].astype(BF16), xf, ln1_g[l], ln1_b[l])
        wg = jnp.pad(w_gate[l].astype(BF16), ((0, 0), (0, pad)))
        wu = jnp.pad(w_up[l].astype(BF16), ((0, 0), (0, pad)))
        wd = jnp.pad(w_down[l].astype(BF16), ((0, pad), (0, 0)))
        hid = _ffn_up(xb, wg, wu)
        xf, xb = _ffn_down(hid, wd, xf, ln2_g[l], ln2_b[l])
    return xf.reshape(bsz, seq, d)
```

```python
import functools
import math

import numpy as np
import jax
import jax.numpy as jnp
from jax import lax
from jax.experimental import pallas as pl
from jax.experimental.pallas import tpu as pltpu

F32 = jnp.float32
BF16 = jnp.bfloat16

D_MODEL = 4096
DEPTH = 4
GW = D_MODEL // 4
LANES = 128
SGU_CHUNK = 128
SGU_HEADS = 8
CONV_WIDTH = 31
CONV_HALO = 32
CONV_ROWS = 128
HGRN_CHUNK = 128
HGRN_SUB = 8
HGRN_LEVELS = (64, 32, 16, 8)
HGRN_STEP_HEADS = 2
ATTN_HEAD_DIM = 64
ATTN_Q_HEADS = 16
ATTN_KV_HEADS = 2
WINDOW = 128
ROPE_DIM = 16
ROPE_THETA = 500000.0
FFN_HIDDEN = 11008
FFN_PAD = 11264
PROJ_COLS = 512
PROJ_ROWS = 64
IN_COLS = 2 * GW + 2 * GW + 4 * GW + (ATTN_Q_HEADS + 2 * ATTN_KV_HEADS) * ATTN_HEAD_DIM
ALPHA = (2 * DEPTH) ** 0.25
LN_EPS = 1e-5
VMEM_LIMIT = 56 * 1024 * 1024


def _params(sem):
    return pltpu.CompilerParams(dimension_semantics=sem, vmem_limit_bytes=VMEM_LIMIT)


def _sigmoid(x):
    return 1.0 / (1.0 + jnp.exp(-x))


def _gelu(x):
    return 0.5 * x * (1.0 + lax.erf(x * (1.0 / math.sqrt(2.0))))


def _layer_norm(x, g, b):
    mu = jnp.mean(x, axis=-1, keepdims=True)
    xc = x - mu
    var = jnp.mean(xc * xc, axis=-1, keepdims=True)
    return xc * lax.rsqrt(var + LN_EPS) * g + b


def _dot(a, b):
    return jnp.dot(a, b, preferred_element_type=F32)


def _dot_nt(a, b):
    return lax.dot_general(a, b, (((1,), (1,)), ((), ())), preferred_element_type=F32)


def _dot_tn(a, b):
    return lax.dot_general(a, b, (((0,), (0,)), ((), ())), preferred_element_type=F32)


def _emb_ln_kernel(x_ref, g_ref, b_ref, o_ref, ob_ref):
    y = _layer_norm(x_ref[...], g_ref[...], b_ref[...])
    o_ref[...] = y
    ob_ref[...] = y.astype(BF16)


def _emb_ln(x, g, b, tm=256):
    n, d = x.shape
    return pl.pallas_call(
        _emb_ln_kernel,
        grid=(n // tm,),
        in_specs=[pl.BlockSpec((tm, d), lambda i: (i, 0)),
                  pl.BlockSpec((1, d), lambda i: (0, 0)),
                  pl.BlockSpec((1, d), lambda i: (0, 0))],
        out_specs=[pl.BlockSpec((tm, d), lambda i: (i, 0)),
                   pl.BlockSpec((tm, d), lambda i: (i, 0))],
        out_shape=[jax.ShapeDtypeStruct((n, d), F32), jax.ShapeDtypeStruct((n, d), BF16)],
        compiler_params=_params(("arbitrary",)),
        name="emb_ln",
    )(x, g.reshape(1, d), b.reshape(1, d))


def _in_proj_kernel(a_ref, w_ref, o_ref):
    o_ref[...] = _dot(a_ref[...], w_ref[...].astype(BF16))


def _in_proj(xb, w, layer, tm=1024, tn=512):
    n, d = xb.shape
    cols = w.shape[2]
    tm = min(tm, n)
    return pl.pallas_call(
        _in_proj_kernel,
        grid=(n // tm, pl.cdiv(cols, tn)),
        in_specs=[pl.BlockSpec((tm, d), lambda i, j: (i, 0)),
                  pl.BlockSpec((None, d, tn), lambda i, j: (layer, 0, j))],
        out_specs=pl.BlockSpec((tm, tn), lambda i, j: (i, j)),
        out_shape=jax.ShapeDtypeStruct((n, cols), F32),
        compiler_params=_params(("arbitrary", "arbitrary")),
        name="in_proj",
    )(xb, w)


def _sgu_kernel(u_ref, v_ref, g_ref, b_ref, w_ref, bs_ref, o_ref, *, tt):
    row = lax.broadcasted_iota(jnp.int32, (SGU_CHUNK, SGU_CHUNK), 0)
    col = lax.broadcasted_iota(jnp.int32, (SGU_CHUNK, SGU_CHUNK), 1)
    causal = row >= col
    for c in range(tt // SGU_CHUNK):
        rows = slice(c * SGU_CHUNK, (c + 1) * SGU_CHUNK)
        u = _gelu(u_ref[rows, :])
        v = _layer_norm(_gelu(v_ref[rows, :]), g_ref[...], b_ref[...]).astype(BF16)
        for hd in range(SGU_HEADS):
            cols = slice(hd * LANES, (hd + 1) * LANES)
            w = jnp.where(causal, w_ref[hd], 0.0).astype(BF16)
            mixed = _dot(w, v[:, cols]) + bs_ref[:, hd:hd + 1]
            o_ref[rows, cols] = (u[:, cols] * mixed).astype(o_ref.dtype)


def _sgu(h, ln_g, ln_b, w_s, b_s, tt=256):
    n = h.shape[0]
    return pl.pallas_call(
        functools.partial(_sgu_kernel, tt=tt),
        grid=(n // tt,),
        in_specs=[pl.BlockSpec((tt, GW), lambda i: (i, 0)),
                  pl.BlockSpec((tt, GW), lambda i: (i, 1)),
                  pl.BlockSpec((1, GW), lambda i: (0, 0)),
                  pl.BlockSpec((1, GW), lambda i: (0, 0)),
                  pl.BlockSpec((SGU_HEADS, SGU_CHUNK, SGU_CHUNK), lambda i: (0, 0, 0)),
                  pl.BlockSpec((SGU_CHUNK, SGU_HEADS), lambda i: (0, 0))],
        out_specs=pl.BlockSpec((tt, GW), lambda i: (i, 0)),
        out_shape=jax.ShapeDtypeStruct((n, GW), BF16),
        compiler_params=_params(("arbitrary",)),
        name="sgu_mixer",
    )(h, h, ln_g.reshape(1, GW), ln_b.reshape(1, GW), w_s, b_s.T)


def _conv_kernel(a_ref, gate_ref, dww_ref, dwb_ref, ng_ref, nb_ref, pww_ref, pwb_ref, o_ref,
                 hbuf, act, *, tt):
    @pl.when(pl.program_id(1) == 0)
    def _():
        hbuf[:, 0:CONV_HALO, :] = jnp.zeros((GW // LANES, CONV_HALO, LANES), F32)

    first = CONV_HALO - (CONV_WIDTH - 1)
    for c in range(GW // LANES):
        cols = slice(c * LANES, (c + 1) * LANES)
        hbuf[c, CONV_HALO:CONV_HALO + tt, :] = a_ref[:, cols] * _sigmoid(gate_ref[:, cols])
        for r0 in range(0, tt, CONV_ROWS):
            acc = jnp.broadcast_to(dwb_ref[:, cols], (CONV_ROWS, LANES))
            for k in range(CONV_WIDTH):
                acc = acc + dww_ref[k:k + 1, cols] * hbuf[c, r0 + first + k:r0 + first + k + CONV_ROWS, :]
            mu = jnp.mean(acc, axis=-1, keepdims=True)
            xc = acc - mu
            var = jnp.mean(xc * xc, axis=-1, keepdims=True)
            hn = xc * lax.rsqrt(var + LN_EPS) * ng_ref[:, cols] + nb_ref[:, cols]
            act[r0:r0 + CONV_ROWS, cols] = (hn * _sigmoid(hn)).astype(BF16)
        hbuf[c, 0:CONV_HALO, :] = hbuf[c, tt:tt + CONV_HALO, :]
    o_ref[...] = (_dot(act[...], pww_ref[...]) + pwb_ref[...]).astype(o_ref.dtype)


def _conv(h, bsz, dw_w, dw_b, norm_g, norm_b, pw_w, pw_b, tt=256):
    n = h.shape[0]
    nt = n // bsz // tt
    row = lambda b, t: b * nt + t
    vec = lambda: pl.BlockSpec((1, GW), lambda b, t: (0, 0))
    return pl.pallas_call(
        functools.partial(_conv_kernel, tt=tt),
        grid=(bsz, nt),
        in_specs=[pl.BlockSpec((tt, GW), lambda b, t: (row(b, t), 2)),
                  pl.BlockSpec((tt, GW), lambda b, t: (row(b, t), 3)),
                  pl.BlockSpec((CONV_WIDTH, GW), lambda b, t: (0, 0)),
                  vec(), vec(), vec(),
                  pl.BlockSpec((GW, GW), lambda b, t: (0, 0)),
                  vec()],
        out_specs=pl.BlockSpec((tt, GW), lambda b, t: (row(b, t), 0)),
        out_shape=jax.ShapeDtypeStruct((n, GW), BF16),
        scratch_shapes=[pltpu.VMEM((GW // LANES, CONV_HALO + tt, LANES), F32), pltpu.VMEM((tt, GW), BF16)],
        compiler_params=_params(("arbitrary", "arbitrary")),
        name="conv_mixer",
    )(h, h, dw_w, dw_b.reshape(1, GW), norm_g.reshape(1, GW), norm_b.reshape(1, GW),
      pw_w, pw_b.reshape(1, GW))


def _hgrn_constants():
    t = np.arange(HGRN_CHUNK)[:, None]
    j = np.arange(HGRN_CHUNK)[None, :]
    tril = (t >= j).astype(np.float32)
    causal = t >= j
    levels = [causal & ((t ^ j) >= m) & ((t ^ j) < 2 * m) for m in HGRN_LEVELS]
    shifts = [causal & ((t // HGRN_SUB) == (j // HGRN_SUB)) & (t - j == s) for s in range(HGRN_SUB)]
    masks = np.stack(levels + shifts).astype(np.float32)
    return jnp.asarray(tril, BF16), jnp.asarray(masks, BF16)


def _hgrn_chunk(zq, zf, zi, zg, lb, ng, state, tril, masks_ref, ones, bbuf, kbuf, pbuf):
    c = HGRN_CHUNK
    q = zq * _sigmoid(zq)
    f = lb + (1.0 - lb) * _sigmoid(zf)
    lf = jnp.log(f) * (1.0 / math.log(2.0))
    k = 1.0 - f
    v = zi.astype(BF16)
    hi = lf.astype(BF16)
    r1 = lf - hi.astype(F32)
    mid = r1.astype(BF16)
    lo = (r1 - mid.astype(F32)).astype(BF16)
    b = _dot(tril, hi) + _dot(tril, mid) + _dot(tril, lo)
    bbuf[HGRN_SUB:HGRN_SUB + c, :] = b
    kbuf[HGRN_SUB:HGRN_SUB + c, :] = k

    o = _dot_nt((q * jnp.exp2(b)).astype(BF16), state.astype(BF16))

    parts = []
    for m in HGRN_LEVELS:
        refs = [jnp.broadcast_to(bbuf[HGRN_SUB + g * 2 * m + m - 1:HGRN_SUB + g * 2 * m + m, :], (2 * m, LANES))
                for g in range(c // (2 * m))]
        ref = refs[0] if len(refs) == 1 else jnp.concatenate(refs, axis=0)
        e = jnp.exp2(-jnp.abs(b - ref))
        parts.append(_dot_nt((q * e).astype(BF16), (k * e).astype(BF16)))

    for s in range(HGRN_SUB):
        bs = bbuf[HGRN_SUB - s:HGRN_SUB - s + c, :]
        ks = kbuf[HGRN_SUB - s:HGRN_SUB - s + c, :]
        p = q * ks * jnp.exp2(jnp.minimum(b - bs, 0.0))
        pbuf[s * c:(s + 1) * c, :] = p.astype(BF16)
    near = _dot(pbuf[...], ones)
    parts += [near[s * c:(s + 1) * c, :] for s in range(HGRN_SUB)]

    scores = masks_ref[0] * parts[0].astype(BF16)
    for i in range(1, len(parts)):
        scores = scores + masks_ref[i] * parts[i].astype(BF16)
    o = o + _dot(scores, v)

    b_end = b[c - 1:c, :]
    new_state = jnp.exp2(b_end) * state + _dot_tn(v, (k * jnp.exp2(b_end - b)).astype(BF16))

    o = o * lax.rsqrt(jnp.mean(o * o, axis=-1, keepdims=True) + LN_EPS)
    return o * ng * _sigmoid(zg), new_state


def _hgrn_kernel(zq_ref, zf_ref, zi_ref, zg_ref, lb_ref, ng_ref, tril_ref, masks_ref, o_ref,
                 state, bbuf, kbuf, pbuf, *, tt):
    @pl.when(pl.program_id(2) == 0)
    def _():
        state[...] = jnp.zeros_like(state)
        bbuf[:, 0:HGRN_SUB, :] = jnp.zeros((HGRN_STEP_HEADS, HGRN_SUB, LANES), F32)
        kbuf[:, 0:HGRN_SUB, :] = jnp.zeros((HGRN_STEP_HEADS, HGRN_SUB, LANES), F32)

    ones = jnp.ones((LANES, LANES), BF16)
    for ci in range(tt // HGRN_CHUNK):
        rows = slice(ci * HGRN_CHUNK, (ci + 1) * HGRN_CHUNK)
        for hh in range(HGRN_STEP_HEADS):
            cols = slice(hh * LANES, (hh + 1) * LANES)
            o, st = _hgrn_chunk(zq_ref[rows, cols], zf_ref[rows, cols], zi_ref[rows, cols], zg_ref[rows, cols],
                                lb_ref[0, :, cols], ng_ref[0, :, cols], state[hh], tril_ref[...], masks_ref,
                                ones, bbuf.at[hh], kbuf.at[hh], pbuf.at[hh])
            state[hh] = st
            o_ref[rows, cols] = o.astype(o_ref.dtype)


def _hgrn(h, bsz, lower_bound, norm_g, tt=512):
    n = h.shape[0]
    nt = n // bsz // tt
    tril, masks = _hgrn_constants()
    width = HGRN_STEP_HEADS * LANES
    groups = GW // width
    base = 4 * GW // width

    def zspec(part):
        return pl.BlockSpec((tt, width), lambda b, hg, t: (b * nt + t, base + part * groups + hg))

    vec = lambda: pl.BlockSpec((1, 1, width), lambda b, hg, t: (hg, 0, 0))
    return pl.pallas_call(
        functools.partial(_hgrn_kernel, tt=tt),
        grid=(bsz, groups, nt),
        in_specs=[zspec(0), zspec(1), zspec(2), zspec(3), vec(), vec(),
                  pl.BlockSpec((HGRN_CHUNK, HGRN_CHUNK), lambda b, hg, t: (0, 0)),
                  pl.BlockSpec(masks.shape, lambda b, hg, t: (0, 0, 0))],
        out_specs=pl.BlockSpec((tt, width), lambda b, hg, t: (b * nt + t, hg)),
        out_shape=jax.ShapeDtypeStruct((n, GW), BF16),
        scratch_shapes=[pltpu.VMEM((HGRN_STEP_HEADS, LANES, LANES), F32),
                        pltpu.VMEM((HGRN_STEP_HEADS, HGRN_SUB + HGRN_CHUNK, LANES), F32),
                        pltpu.VMEM((HGRN_STEP_HEADS, HGRN_SUB + HGRN_CHUNK, LANES), F32),
                        pltpu.VMEM((HGRN_STEP_HEADS, HGRN_SUB * HGRN_CHUNK, LANES), BF16)],
        compiler_params=_params(("arbitrary", "arbitrary", "arbitrary")),
        name="hgrn_mixer",
    )(h, h, h, h, lower_bound.reshape(groups, 1, width), norm_g.reshape(groups, 1, width), tril, masks)


def _rope_constants():
    lane = np.arange(LANES) % ATTN_HEAD_DIM
    half = ROPE_DIM // 2
    inv_freq = ROPE_THETA ** (-jnp.arange(0, ROPE_DIM, 2, dtype=F32) / ROPE_DIM)
    freq = jnp.where(lane < ROPE_DIM, inv_freq[lane % half], 0.0).astype(F32)
    rot = (lane < ROPE_DIM).astype(np.float32)
    up = ((lane >= half) & (lane < ROPE_DIM)).astype(np.float32)
    dn = (lane < half).astype(np.float32)
    return jnp.stack([freq, jnp.asarray(rot), jnp.asarray(up), jnp.asarray(dn)])[:, None, :]


def _swa_kernel(sink_ref, pos_ref, q_ref, k_ref, v_ref, rc_ref, o_ref, kprev, vprev):
    t = pl.program_id(1)
    w = WINDOW

    @pl.when(t == 0)
    def _():
        kprev[...] = jnp.zeros_like(kprev)
        vprev[...] = jnp.zeros_like(vprev)

    ang = pos_ref[...].astype(F32) * rc_ref[0]
    cos, sin = jnp.cos(ang), jnp.sin(ang)
    c_mul = cos * rc_ref[1] + (1.0 - rc_ref[1])
    s_up = sin * rc_ref[2]
    s_dn = -sin * rc_ref[3]
    half = ROPE_DIM // 2

    def rope(x):
        return x * c_mul + pltpu.roll(x, half, 1) * s_up + pltpu.roll(x, LANES - half, 1) * s_dn

    lane = lax.broadcasted_iota(jnp.int32, (w, LANES), 1)
    low = lane < ATTN_HEAD_DIM
    k_cur = rope(k_ref[...])
    v_cur = v_ref[...]
    k_sw = pltpu.roll(k_cur, ATTN_HEAD_DIM, 1)
    v_sw = pltpu.roll(v_cur, ATTN_HEAD_DIM, 1)
    k_dup = [jnp.where(low, k_cur, k_sw).astype(BF16), jnp.where(low, k_sw, k_cur).astype(BF16)]
    v_dup = [jnp.where(low, v_cur, v_sw).astype(BF16), jnp.where(low, v_sw, v_cur).astype(BF16)]

    qi = lax.broadcasted_iota(jnp.int32, (w, w), 0)
    kj = lax.broadcasted_iota(jnp.int32, (w, w), 1)
    from_prev = kj > qi
    no_prev = jnp.where(t > 0, 0.0, -jnp.inf)
    scale = ATTN_HEAD_DIM ** -0.5
    group = ATTN_Q_HEADS // ATTN_KV_HEADS

    for kv in range(ATTN_KV_HEADS):
        qs = []
        for pair in range(kv * group // 2, (kv + 1) * group // 2):
            qt = rope(q_ref[:, pair * LANES:(pair + 1) * LANES]) * scale
            qs += [jnp.where(low, qt, 0.0).astype(BF16), jnp.where(low, 0.0, qt).astype(BF16)]
        qst = jnp.concatenate(qs, axis=0)
        s_prev = _dot_nt(qst, kprev[kv]).reshape(group, w, w)
        s_cur = _dot_nt(qst, k_dup[kv]).reshape(group, w, w)
        s = jnp.where(from_prev, s_prev + no_prev, s_cur)
        sink = sink_ref[kv * group:(kv + 1) * group]
        m = jnp.maximum(jnp.max(s, axis=-1, keepdims=True), sink)
        p = jnp.exp(s - m)
        denom = jnp.sum(p, axis=-1, keepdims=True) + jnp.exp(sink - m)
        p_prev = jnp.where(from_prev, p, 0.0).astype(BF16).reshape(group * w, w)
        p_cur = jnp.where(from_prev, 0.0, p).astype(BF16).reshape(group * w, w)
        acc = (_dot(p_prev, vprev[kv]) + _dot(p_cur, v_dup[kv])).reshape(group, w, LANES) / denom
        for g in range(0, group, 2):
            pair = (kv * group + g) // 2
            o_ref[:, pair * LANES:(pair + 1) * LANES] = jnp.where(low, acc[g], acc[g + 1]).astype(o_ref.dtype)

    for kv in range(ATTN_KV_HEADS):
        kprev[kv] = k_dup[kv]
        vprev[kv] = v_dup[kv]


def _swa(h, bsz, positions, sinks):
    n = h.shape[0]
    nt = n // bsz // WINDOW
    row = lambda b, t: b * nt + t
    qblk = 8
    kblk = (9 * GW) // LANES
    return pl.pallas_call(
        _swa_kernel,
        grid=(bsz, nt),
        in_specs=[pl.BlockSpec((ATTN_Q_HEADS, 1, 1), lambda b, t: (0, 0, 0)),
                  pl.BlockSpec((WINDOW, 1), lambda b, t: (row(b, t), 0)),
                  pl.BlockSpec((WINDOW, GW), lambda b, t: (row(b, t), qblk)),
                  pl.BlockSpec((WINDOW, LANES), lambda b, t: (row(b, t), kblk)),
                  pl.BlockSpec((WINDOW, LANES), lambda b, t: (row(b, t), kblk + 1)),
                  pl.BlockSpec((4, 1, LANES), lambda b, t: (0, 0, 0))],
        out_specs=pl.BlockSpec((WINDOW, GW), lambda b, t: (row(b, t), 0)),
        out_shape=jax.ShapeDtypeStruct((n, GW), BF16),
        scratch_shapes=[pltpu.VMEM((ATTN_KV_HEADS, WINDOW, LANES), BF16),
                        pltpu.VMEM((ATTN_KV_HEADS, WINDOW, LANES), BF16)],
        compiler_params=_params(("arbitrary", "arbitrary")),
        name="swa_mixer",
    )(sinks.astype(F32).reshape(ATTN_Q_HEADS, 1, 1), positions.reshape(n, 1), h, h, h, _rope_constants())


def _proj_ln_kernel(a_ref, w_ref, x_ref, g_ref, b_ref, o_ref, ob_ref):
    k = pl.program_id(1)
    tm, d = o_ref.shape
    col_chunks = [slice(c, c + PROJ_COLS) for c in range(0, d, PROJ_COLS)]

    @pl.when(k == 0)
    def _():
        for cols in col_chunks:
            o_ref[:, cols] = ALPHA * x_ref[:, cols] + _dot(a_ref[...], w_ref[:, cols])

    @pl.when(k > 0)
    def _():
        for cols in col_chunks:
            o_ref[:, cols] += _dot(a_ref[...], w_ref[:, cols])

    @pl.when(k == pl.num_programs(1) - 1)
    def _():
        def slab(r, carry):
            rows = pl.ds(pl.multiple_of(r * PROJ_ROWS, PROJ_ROWS), PROJ_ROWS)
            y = _layer_norm(o_ref[rows, :], g_ref[...], b_ref[...])
            o_ref[rows, :] = y
            ob_ref[rows, :] = y.astype(BF16)
            return carry

        lax.fori_loop(0, tm // PROJ_ROWS, slab, 0)


def _proj_ln(a, w, layer, x, g, b, name, tm=512, tk=512):
    n, d = x.shape
    kdim = a.shape[1]
    tm = min(tm, n)
    assert kdim // tk >= 2 and d % PROJ_COLS == 0 and tm % PROJ_ROWS == 0
    row = lambda: pl.BlockSpec((tm, d), lambda i, k: (i, 0))
    vec = lambda: pl.BlockSpec((1, d), lambda i, k: (0, 0))
    return pl.pallas_call(
        _proj_ln_kernel,
        grid=(n // tm, kdim // tk),
        in_specs=[pl.BlockSpec((tm, tk), lambda i, k: (i, k)),
                  pl.BlockSpec((None, tk, d), lambda i, k: (layer, k, 0)),
                  pl.BlockSpec((tm, d), lambda i, k: (i, 0), pipeline_mode=pl.Buffered(1)),
                  vec(), vec()],
        out_specs=[row(), row()],
        out_shape=[jax.ShapeDtypeStruct((n, d), F32), jax.ShapeDtypeStruct((n, d), BF16)],
        compiler_params=_params(("arbitrary", "arbitrary")),
        name=name,
    )(a, w, x, g.reshape(1, d), b.reshape(1, d))


def _ffn_up_kernel(x_ref, wg_ref, wu_ref, o_ref):
    last = pl.num_programs(1) - 1

    @pl.when(pl.program_id(1) < last)
    def _():
        x = x_ref[...]
        gate = _dot(x, wg_ref[...].astype(BF16))
        up = _dot(x, wu_ref[...].astype(BF16))
        o_ref[...] = (gate * _sigmoid(gate) * up).astype(o_ref.dtype)

    @pl.when(pl.program_id(1) == last)
    def _():
        o_ref[...] = jnp.zeros_like(o_ref)


def _ffn_up(xb, wg, wu, layer, tm=1024, tn=256):
    n, d = xb.shape
    f = wg.shape[2]
    assert f % tn == 0 and FFN_PAD - f == tn
    tm = min(tm, n)
    nj = f // tn
    wspec = lambda: pl.BlockSpec((None, d, tn), lambda i, j: (layer, 0, jnp.minimum(j, nj - 1)))
    return pl.pallas_call(
        _ffn_up_kernel,
        grid=(n // tm, nj + 1),
        in_specs=[pl.BlockSpec((tm, d), lambda i, j: (i, 0)), wspec(), wspec()],
        out_specs=pl.BlockSpec((tm, tn), lambda i, j: (i, j)),
        out_shape=jax.ShapeDtypeStruct((n, FFN_PAD), BF16),
        compiler_params=_params(("arbitrary", "arbitrary")),
        name="ffn_up",
    )(xb, wg, wu)


def kernel(x, positions, emb_ln_g, emb_ln_b, w_in, sgu_ln_g, sgu_ln_b, sgu_w, sgu_b, conv_dw_w, conv_dw_b, conv_norm_g, conv_norm_b, conv_pw_w, conv_pw_b, hgrn_lower_bound, hgrn_norm_g, attn_sinks, w_o, ln1_g, ln1_b, w_gate, w_up, w_down, ln2_g, ln2_b):
    bsz, seq, d = x.shape
    n = bsz * seq
    depth = w_in.shape[0]
    p_lb = jax.nn.softmax(hgrn_lower_bound.astype(F32), axis=0)
    lower_bounds = jnp.cumsum(p_lb, axis=0) - p_lb[0]
    wo_b = w_o.astype(BF16)
    wd_b = jnp.pad(w_down.astype(BF16), ((0, 0), (0, FFN_PAD - w_down.shape[1]), (0, 0)))
    pw_b = conv_pw_w.astype(BF16)

    xf, xb = _emb_ln(x.reshape(n, d), emb_ln_g, emb_ln_b)
    for l in range(depth):
        h = _in_proj(xb, w_in, l)
        y_a = _sgu(h, sgu_ln_g[l], sgu_ln_b[l], sgu_w[l], sgu_b[l])
        y_b = _conv(h, bsz, conv_dw_w[l], conv_dw_b[l], conv_norm_g[l], conv_norm_b[l], pw_b[l], conv_pw_b[l])
        y_c = _hgrn(h, bsz, lower_bounds[l], hgrn_norm_g[l])
        y_d = _swa(h, bsz, positions, attn_sinks[l])
        y = jnp.concatenate([y_a, y_b, y_c, y_d], axis=-1)
        xf, xb = _proj_ln(y, wo_b, l, xf, ln1_g[l], ln1_b[l], "out_proj_ln")
        hid = _ffn_up(xb, w_gate, w_up, l)
        xf, xb = _proj_ln(hid, wd_b, l, xf, ln2_g[l], ln2_b[l], "ffn_down_ln", tk=1024)
    return xf.reshape(bsz, seq, d)
```

```python
import functools
import math

import numpy as np
import jax
import jax.numpy as jnp
from jax import lax
from jax.experimental import pallas as pl
from jax.experimental.pallas import tpu as pltpu

F32 = jnp.float32
BF16 = jnp.bfloat16

D_MODEL = 4096
DEPTH = 4
GW = D_MODEL // 4
LANES = 128
SGU_CHUNK = 128
SGU_HEADS = 8
CONV_WIDTH = 31
CONV_HALO = 32
CONV_ROWS = 128
HGRN_CHUNK = 128
HGRN_SUB = 8
HGRN_LEVELS = (64, 32, 16, 8)
HGRN_STEP_HEADS = 2
ATTN_HEAD_DIM = 64
ATTN_Q_HEADS = 16
ATTN_KV_HEADS = 2
WINDOW = 128
ROPE_DIM = 16
ROPE_THETA = 500000.0
FFN_HIDDEN = 11008
FFN_PAD = 11264
PROJ_COLS = 512
PROJ_ROWS = 64
IN_COLS = 2 * GW + 2 * GW + 4 * GW + (ATTN_Q_HEADS + 2 * ATTN_KV_HEADS) * ATTN_HEAD_DIM
ALPHA = (2 * DEPTH) ** 0.25
LN_EPS = 1e-5
VMEM_LIMIT = 60 * 1024 * 1024


def _params(sem):
    return pltpu.CompilerParams(dimension_semantics=sem, vmem_limit_bytes=VMEM_LIMIT)


def _sigmoid(x):
    return 1.0 / (1.0 + jnp.exp(-x))


def _gelu(x):
    return 0.5 * x * (1.0 + lax.erf(x * (1.0 / math.sqrt(2.0))))


def _layer_norm(x, g, b):
    mu = jnp.mean(x, axis=-1, keepdims=True)
    xc = x - mu
    var = jnp.mean(xc * xc, axis=-1, keepdims=True)
    return xc * lax.rsqrt(var + LN_EPS) * g + b


def _dot(a, b):
    return jnp.dot(a, b, preferred_element_type=F32)


def _dot_nt(a, b):
    return lax.dot_general(a, b, (((1,), (1,)), ((), ())), preferred_element_type=F32)


def _dot_tn(a, b):
    return lax.dot_general(a, b, (((0,), (0,)), ((), ())), preferred_element_type=F32)


def _emb_ln_kernel(x_ref, g_ref, b_ref, o_ref, ob_ref):
    y = _layer_norm(x_ref[...], g_ref[...], b_ref[...])
    o_ref[...] = y
    ob_ref[...] = y.astype(BF16)


def _emb_ln(x, g, b, tm=256):
    n, d = x.shape
    return pl.pallas_call(
        _emb_ln_kernel,
        grid=(n // tm,),
        in_specs=[pl.BlockSpec((tm, d), lambda i: (i, 0)),
                  pl.BlockSpec((1, d), lambda i: (0, 0)),
                  pl.BlockSpec((1, d), lambda i: (0, 0))],
        out_specs=[pl.BlockSpec((tm, d), lambda i: (i, 0)),
                   pl.BlockSpec((tm, d), lambda i: (i, 0))],
        out_shape=[jax.ShapeDtypeStruct((n, d), F32), jax.ShapeDtypeStruct((n, d), BF16)],
        compiler_params=_params(("arbitrary",)),
        name="emb_ln",
    )(x, g.reshape(1, d), b.reshape(1, d))


def _in_proj_kernel(a_ref, w_ref, o_ref):
    o_ref[...] = _dot(a_ref[...], w_ref[...].astype(BF16))


def _in_proj(xb, w, layer, tm=2048, tn=512):
    n, d = xb.shape
    cols = w.shape[2]
    tm = min(tm, n)
    return pl.pallas_call(
        _in_proj_kernel,
        grid=(n // tm, pl.cdiv(cols, tn)),
        in_specs=[pl.BlockSpec((tm, d), lambda i, j: (i, 0), pipeline_mode=pl.Buffered(1)),
                  pl.BlockSpec((None, d, tn), lambda i, j: (layer, 0, j))],
        out_specs=pl.BlockSpec((tm, tn), lambda i, j: (i, j)),
        out_shape=jax.ShapeDtypeStruct((n, cols), F32),
        compiler_params=_params(("arbitrary", "arbitrary")),
        name="in_proj",
    )(xb, w)


def _sgu_kernel(u_ref, v_ref, g_ref, b_ref, w_ref, bs_ref, o_ref, *, tt):
    row = lax.broadcasted_iota(jnp.int32, (SGU_CHUNK, SGU_CHUNK), 0)
    col = lax.broadcasted_iota(jnp.int32, (SGU_CHUNK, SGU_CHUNK), 1)
    causal = row >= col
    for c in range(tt // SGU_CHUNK):
        rows = slice(c * SGU_CHUNK, (c + 1) * SGU_CHUNK)
        u = _gelu(u_ref[rows, :])
        v = _layer_norm(_gelu(v_ref[rows, :]), g_ref[...], b_ref[...]).astype(BF16)
        for hd in range(SGU_HEADS):
            cols = slice(hd * LANES, (hd + 1) * LANES)
            w = jnp.where(causal, w_ref[hd], 0.0).astype(BF16)
            mixed = _dot(w, v[:, cols]) + bs_ref[:, hd:hd + 1]
            o_ref[rows, cols] = (u[:, cols] * mixed).astype(o_ref.dtype)


def _sgu(h, ln_g, ln_b, w_s, b_s, tt=256):
    n = h.shape[0]
    return pl.pallas_call(
        functools.partial(_sgu_kernel, tt=tt),
        grid=(n // tt,),
        in_specs=[pl.BlockSpec((tt, GW), lambda i: (i, 0)),
                  pl.BlockSpec((tt, GW), lambda i: (i, 1)),
                  pl.BlockSpec((1, GW), lambda i: (0, 0)),
                  pl.BlockSpec((1, GW), lambda i: (0, 0)),
                  pl.BlockSpec((SGU_HEADS, SGU_CHUNK, SGU_CHUNK), lambda i: (0, 0, 0)),
                  pl.BlockSpec((SGU_CHUNK, SGU_HEADS), lambda i: (0, 0))],
        out_specs=pl.BlockSpec((tt, GW), lambda i: (i, 0)),
        out_shape=jax.ShapeDtypeStruct((n, GW), BF16),
        compiler_params=_params(("arbitrary",)),
        name="sgu_mixer",
    )(h, h, ln_g.reshape(1, GW), ln_b.reshape(1, GW), w_s, b_s.T)


def _conv_kernel(a_ref, gate_ref, dww_ref, dwb_ref, ng_ref, nb_ref, pww_ref, pwb_ref, o_ref,
                 hbuf, act, *, tt):
    @pl.when(pl.program_id(1) == 0)
    def _():
        hbuf[:, 0:CONV_HALO, :] = jnp.zeros((GW // LANES, CONV_HALO, LANES), F32)

    first = CONV_HALO - (CONV_WIDTH - 1)
    for c in range(GW // LANES):
        cols = slice(c * LANES, (c + 1) * LANES)
        hbuf[c, CONV_HALO:CONV_HALO + tt, :] = a_ref[:, cols] * _sigmoid(gate_ref[:, cols])
        for r0 in range(0, tt, CONV_ROWS):
            acc = jnp.broadcast_to(dwb_ref[:, cols], (CONV_ROWS, LANES))
            for k in range(CONV_WIDTH):
                acc = acc + dww_ref[k:k + 1, cols] * hbuf[c, r0 + first + k:r0 + first + k + CONV_ROWS, :]
            mu = jnp.mean(acc, axis=-1, keepdims=True)
            xc = acc - mu
            var = jnp.mean(xc * xc, axis=-1, keepdims=True)
            hn = xc * lax.rsqrt(var + LN_EPS) * ng_ref[:, cols] + nb_ref[:, cols]
            act[r0:r0 + CONV_ROWS, cols] = (hn * _sigmoid(hn)).astype(BF16)
        hbuf[c, 0:CONV_HALO, :] = hbuf[c, tt:tt + CONV_HALO, :]
    o_ref[...] = (_dot(act[...], pww_ref[...]) + pwb_ref[...]).astype(o_ref.dtype)


def _conv(h, bsz, dw_w, dw_b, norm_g, norm_b, pw_w, pw_b, tt=256):
    n = h.shape[0]
    nt = n // bsz // tt
    row = lambda b, t: b * nt + t
    vec = lambda: pl.BlockSpec((1, GW), lambda b, t: (0, 0))
    return pl.pallas_call(
        functools.partial(_conv_kernel, tt=tt),
        grid=(bsz, nt),
        in_specs=[pl.BlockSpec((tt, GW), lambda b, t: (row(b, t), 2)),
                  pl.BlockSpec((tt, GW), lambda b, t: (row(b, t), 3)),
                  pl.BlockSpec((CONV_WIDTH, GW), lambda b, t: (0, 0)),
                  vec(), vec(), vec(),
                  pl.BlockSpec((GW, GW), lambda b, t: (0, 0)),
                  vec()],
        out_specs=pl.BlockSpec((tt, GW), lambda b, t: (row(b, t), 0)),
        out_shape=jax.ShapeDtypeStruct((n, GW), BF16),
        scratch_shapes=[pltpu.VMEM((GW // LANES, CONV_HALO + tt, LANES), F32), pltpu.VMEM((tt, GW), BF16)],
        compiler_params=_params(("arbitrary", "arbitrary")),
        name="conv_mixer",
    )(h, h, dw_w, dw_b.reshape(1, GW), norm_g.reshape(1, GW), norm_b.reshape(1, GW),
      pw_w, pw_b.reshape(1, GW))


def _hgrn_constants():
    t = np.arange(HGRN_CHUNK)[:, None]
    j = np.arange(HGRN_CHUNK)[None, :]
    tril = (t >= j).astype(np.float32)
    causal = t >= j
    levels = [causal & ((t ^ j) >= m) & ((t ^ j) < 2 * m) for m in HGRN_LEVELS]
    shifts = [causal & ((t // HGRN_SUB) == (j // HGRN_SUB)) & (t - j == s) for s in range(HGRN_SUB)]
    masks = np.stack(levels + shifts).astype(np.float32)
    return jnp.asarray(tril, BF16), jnp.asarray(masks, BF16)


def _hgrn_chunk(zq, zf, zi, zg, lb, ng, state, tril, masks_ref, ones, bbuf, kbuf, pbuf):
    c = HGRN_CHUNK
    q = zq * _sigmoid(zq)
    f = lb + (1.0 - lb) * _sigmoid(zf)
    lf = jnp.log(f) * (1.0 / math.log(2.0))
    k = 1.0 - f
    v = zi.astype(BF16)
    hi = lf.astype(BF16)
    r1 = lf - hi.astype(F32)
    mid = r1.astype(BF16)
    lo = (r1 - mid.astype(F32)).astype(BF16)
    b = _dot(tril, hi) + _dot(tril, mid) + _dot(tril, lo)
    bbuf[HGRN_SUB:HGRN_SUB + c, :] = b
    kbuf[HGRN_SUB:HGRN_SUB + c, :] = k

    o = _dot_nt((q * jnp.exp2(b)).astype(BF16), state.astype(BF16))

    parts = []
    for m in HGRN_LEVELS:
        refs = [jnp.broadcast_to(bbuf[HGRN_SUB + g * 2 * m + m - 1:HGRN_SUB + g * 2 * m + m, :], (2 * m, LANES))
                for g in range(c // (2 * m))]
        ref = refs[0] if len(refs) == 1 else jnp.concatenate(refs, axis=0)
        e = jnp.exp2(-jnp.abs(b - ref))
        parts.append(_dot_nt((q * e).astype(BF16), (k * e).astype(BF16)))

    for s in range(HGRN_SUB):
        bs = bbuf[HGRN_SUB - s:HGRN_SUB - s + c, :]
        ks = kbuf[HGRN_SUB - s:HGRN_SUB - s + c, :]
        p = q * ks * jnp.exp2(jnp.minimum(b - bs, 0.0))
        pbuf[s * c:(s + 1) * c, :] = p.astype(BF16)
    near = _dot(pbuf[...], ones)
    parts += [near[s * c:(s + 1) * c, :] for s in range(HGRN_SUB)]

    scores = masks_ref[0] * parts[0].astype(BF16)
    for i in range(1, len(parts)):
        scores = scores + masks_ref[i] * parts[i].astype(BF16)
    o = o + _dot(scores, v)

    b_end = b[c - 1:c, :]
    new_state = jnp.exp2(b_end) * state + _dot_tn(v, (k * jnp.exp2(b_end - b)).astype(BF16))

    o = o * lax.rsqrt(jnp.mean(o * o, axis=-1, keepdims=True) + LN_EPS)
    return o * ng * _sigmoid(zg), new_state


def _hgrn_kernel(zq_ref, zf_ref, zi_ref, zg_ref, lb_ref, ng_ref, tril_ref, masks_ref, o_ref,
                 state, bbuf, kbuf, pbuf, *, tt):
    @pl.when(pl.program_id(2) == 0)
    def _():
        state[...] = jnp.zeros_like(state)
        bbuf[:, 0:HGRN_SUB, :] = jnp.zeros((HGRN_STEP_HEADS, HGRN_SUB, LANES), F32)
        kbuf[:, 0:HGRN_SUB, :] = jnp.zeros((HGRN_STEP_HEADS, HGRN_SUB, LANES), F32)

    ones = jnp.ones((LANES, LANES), BF16)
    for ci in range(tt // HGRN_CHUNK):
        rows = slice(ci * HGRN_CHUNK, (ci + 1) * HGRN_CHUNK)
        for hh in range(HGRN_STEP_HEADS):
            cols = slice(hh * LANES, (hh + 1) * LANES)
            o, st = _hgrn_chunk(zq_ref[rows, cols], zf_ref[rows, cols], zi_ref[rows, cols], zg_ref[rows, cols],
                                lb_ref[0, :, cols], ng_ref[0, :, cols], state[hh], tril_ref[...], masks_ref,
                                ones, bbuf.at[hh], kbuf.at[hh], pbuf.at[hh])
            state[hh] = st
            o_ref[rows, cols] = o.astype(o_ref.dtype)


def _hgrn(h, bsz, lower_bound, norm_g, tt=512):
    n = h.shape[0]
    nt = n // bsz // tt
    tril, masks = _hgrn_constants()
    width = HGRN_STEP_HEADS * LANES
    groups = GW // width
    base = 4 * GW // width

    def zspec(part):
        return pl.BlockSpec((tt, width), lambda b, hg, t: (b * nt + t, base + part * groups + hg))

    vec = lambda: pl.BlockSpec((1, 1, width), lambda b, hg, t: (hg, 0, 0))
    return pl.pallas_call(
        functools.partial(_hgrn_kernel, tt=tt),
        grid=(bsz, groups, nt),
        in_specs=[zspec(0), zspec(1), zspec(2), zspec(3), vec(), vec(),
                  pl.BlockSpec((HGRN_CHUNK, HGRN_CHUNK), lambda b, hg, t: (0, 0)),
                  pl.BlockSpec(masks.shape, lambda b, hg, t: (0, 0, 0))],
        out_specs=pl.BlockSpec((tt, width), lambda b, hg, t: (b * nt + t, hg)),
        out_shape=jax.ShapeDtypeStruct((n, GW), BF16),
        scratch_shapes=[pltpu.VMEM((HGRN_STEP_HEADS, LANES, LANES), F32),
                        pltpu.VMEM((HGRN_STEP_HEADS, HGRN_SUB + HGRN_CHUNK, LANES), F32),
                        pltpu.VMEM((HGRN_STEP_HEADS, HGRN_SUB + HGRN_CHUNK, LANES), F32),
                        pltpu.VMEM((HGRN_STEP_HEADS, HGRN_SUB * HGRN_CHUNK, LANES), BF16)],
        compiler_params=_params(("arbitrary", "arbitrary", "arbitrary")),
        name="hgrn_mixer",
    )(h, h, h, h, lower_bound.reshape(groups, 1, width), norm_g.reshape(groups, 1, width), tril, masks)


def _rope_constants():
    lane = np.arange(LANES) % ATTN_HEAD_DIM
    half = ROPE_DIM // 2
    inv_freq = ROPE_THETA ** (-jnp.arange(0, ROPE_DIM, 2, dtype=F32) / ROPE_DIM)
    freq = jnp.where(lane < ROPE_DIM, inv_freq[lane % half], 0.0).astype(F32)
    rot = (lane < ROPE_DIM).astype(np.float32)
    up = ((lane >= half) & (lane < ROPE_DIM)).astype(np.float32)
    dn = (lane < half).astype(np.float32)
    return jnp.stack([freq, jnp.asarray(rot), jnp.asarray(up), jnp.asarray(dn)])[:, None, :]


def _swa_kernel(sink_ref, pos_ref, q_ref, k_ref, v_ref, rc_ref, o_ref, kprev, vprev):
    t = pl.program_id(1)
    w = WINDOW

    @pl.when(t == 0)
    def _():
        kprev[...] = jnp.zeros_like(kprev)
        vprev[...] = jnp.zeros_like(vprev)

    ang = pos_ref[...].astype(F32) * rc_ref[0]
    cos, sin = jnp.cos(ang), jnp.sin(ang)
    c_mul = cos * rc_ref[1] + (1.0 - rc_ref[1])
    s_up = sin * rc_ref[2]
    s_dn = -sin * rc_ref[3]
    half = ROPE_DIM // 2

    def rope(x):
        return x * c_mul + pltpu.roll(x, half, 1) * s_up + pltpu.roll(x, LANES - half, 1) * s_dn

    lane = lax.broadcasted_iota(jnp.int32, (w, LANES), 1)
    low = lane < ATTN_HEAD_DIM
    k_cur = rope(k_ref[...])
    v_cur = v_ref[...]
    k_sw = pltpu.roll(k_cur, ATTN_HEAD_DIM, 1)
    v_sw = pltpu.roll(v_cur, ATTN_HEAD_DIM, 1)
    k_dup = [jnp.where(low, k_cur, k_sw).astype(BF16), jnp.where(low, k_sw, k_cur).astype(BF16)]
    v_dup = [jnp.where(low, v_cur, v_sw).astype(BF16), jnp.where(low, v_sw, v_cur).astype(BF16)]

    qi = lax.broadcasted_iota(jnp.int32, (w, w), 0)
    kj = lax.broadcasted_iota(jnp.int32, (w, w), 1)
    from_prev = kj > qi
    no_prev = jnp.where(t > 0, 0.0, -jnp.inf)
    scale = ATTN_HEAD_DIM ** -0.5
    group = ATTN_Q_HEADS // ATTN_KV_HEADS

    for kv in range(ATTN_KV_HEADS):
        qs = []
        for pair in range(kv * group // 2, (kv + 1) * group // 2):
            qt = rope(q_ref[:, pair * LANES:(pair + 1) * LANES]) * scale
            qs += [jnp.where(low, qt, 0.0).astype(BF16), jnp.where(low, 0.0, qt).astype(BF16)]
        qst = jnp.concatenate(qs, axis=0)
        s_prev = _dot_nt(qst, kprev[kv]).reshape(group, w, w)
        s_cur = _dot_nt(qst, k_dup[kv]).reshape(group, w, w)
        s = jnp.where(from_prev, s_prev + no_prev, s_cur)
        sink = sink_ref[kv * group:(kv + 1) * group]
        m = jnp.maximum(jnp.max(s, axis=-1, keepdims=True), sink)
        p = jnp.exp(s - m)
        denom = jnp.sum(p, axis=-1, keepdims=True) + jnp.exp(sink - m)
        p_prev = jnp.where(from_prev, p, 0.0).astype(BF16).reshape(group * w, w)
        p_cur = jnp.where(from_prev, 0.0, p).astype(BF16).reshape(group * w, w)
        acc = (_dot(p_prev, vprev[kv]) + _dot(p_cur, v_dup[kv])).reshape(group, w, LANES) / denom
        for g in range(0, group, 2):
            pair = (kv * group + g) // 2
            o_ref[:, pair * LANES:(pair + 1) * LANES] = jnp.where(low, acc[g], acc[g + 1]).astype(o_ref.dtype)

    for kv in range(ATTN_KV_HEADS):
        kprev[kv] = k_dup[kv]
        vprev[kv] = v_dup[kv]


def _swa(h, bsz, positions, sinks):
    n = h.shape[0]
    nt = n // bsz // WINDOW
    row = lambda b, t: b * nt + t
    qblk = 8
    kblk = (9 * GW) // LANES
    return pl.pallas_call(
        _swa_kernel,
        grid=(bsz, nt),
        in_specs=[pl.BlockSpec((ATTN_Q_HEADS, 1, 1), lambda b, t: (0, 0, 0)),
                  pl.BlockSpec((WINDOW, 1), lambda b, t: (row(b, t), 0)),
                  pl.BlockSpec((WINDOW, GW), lambda b, t: (row(b, t), qblk)),
                  pl.BlockSpec((WINDOW, LANES), lambda b, t: (row(b, t), kblk)),
                  pl.BlockSpec((WINDOW, LANES), lambda b, t: (row(b, t), kblk + 1)),
                  pl.BlockSpec((4, 1, LANES), lambda b, t: (0, 0, 0))],
        out_specs=pl.BlockSpec((WINDOW, GW), lambda b, t: (row(b, t), 0)),
        out_shape=jax.ShapeDtypeStruct((n, GW), BF16),
        scratch_shapes=[pltpu.VMEM((ATTN_KV_HEADS, WINDOW, LANES), BF16),
                        pltpu.VMEM((ATTN_KV_HEADS, WINDOW, LANES), BF16)],
        compiler_params=_params(("arbitrary", "arbitrary")),
        name="swa_mixer",
    )(sinks.astype(F32).reshape(ATTN_Q_HEADS, 1, 1), positions.reshape(n, 1), h, h, h, _rope_constants())


def _proj_ln_kernel(a_ref, w_ref, x_ref, g_ref, b_ref, o_ref, ob_ref):
    k = pl.program_id(1)
    tm, d = o_ref.shape
    col_chunks = [slice(c, c + PROJ_COLS) for c in range(0, d, PROJ_COLS)]

    @pl.when(k == 0)
    def _():
        for cols in col_chunks:
            o_ref[:, cols] = ALPHA * x_ref[:, cols] + _dot(a_ref[...], w_ref[:, cols])

    @pl.when(k > 0)
    def _():
        for cols in col_chunks:
            o_ref[:, cols] += _dot(a_ref[...], w_ref[:, cols])

    @pl.when(k == pl.num_programs(1) - 1)
    def _():
        def slab(r, carry):
            rows = pl.ds(pl.multiple_of(r * PROJ_ROWS, PROJ_ROWS), PROJ_ROWS)
            y = _layer_norm(o_ref[rows, :], g_ref[...], b_ref[...])
            o_ref[rows, :] = y
            ob_ref[rows, :] = y.astype(BF16)
            return carry

        lax.fori_loop(0, tm // PROJ_ROWS, slab, 0)


def _proj_ln(a, w, layer, x, g, b, name, tm=512, tk=512):
    n, d = x.shape
    kdim = a.shape[1]
    tm = min(tm, n)
    assert kdim % tk == 0 and d % PROJ_COLS == 0 and tm % PROJ_ROWS == 0
    row = lambda: pl.BlockSpec((tm, d), lambda i, k: (i, 0))
    vec = lambda: pl.BlockSpec((1, d), lambda i, k: (0, 0))
    once = pl.Buffered(1)
    if kdim == tk:
        w_mode, x_mode = once, None
    else:
        w_mode, x_mode = None, once
    return pl.pallas_call(
        _proj_ln_kernel,
        grid=(n // tm, kdim // tk),
        in_specs=[pl.BlockSpec((tm, tk), lambda i, k: (i, k)),
                  pl.BlockSpec((None, tk, d), lambda i, k: (layer, k, 0), pipeline_mode=w_mode),
                  pl.BlockSpec((tm, d), lambda i, k: (i, 0), pipeline_mode=x_mode),
                  vec(), vec()],
        out_specs=[row(), row()],
        out_shape=[jax.ShapeDtypeStruct((n, d), F32), jax.ShapeDtypeStruct((n, d), BF16)],
        compiler_params=_params(("arbitrary", "arbitrary")),
        name=name,
    )(a, w, x, g.reshape(1, d), b.reshape(1, d))


def _proj_ln_tall_kernel(a_ref, w_ref, x_hbm, g_ref, b_ref, o_hbm, ob_hbm,
                         acc, x_buf, o_buf, ob_buf, x_sem, o_sem, ob_sem):
    i = pl.program_id(0)
    k = pl.program_id(1)
    tm, d = acc.shape
    col_chunks = [slice(c, c + PROJ_COLS) for c in range(0, d, PROJ_COLS)]
    slabs = tm // PROJ_ROWS

    @pl.when(k == 0)
    def _():
        for cols in col_chunks:
            acc[:, cols] = _dot(a_ref[...], w_ref[:, cols])

    @pl.when(k > 0)
    def _():
        for cols in col_chunks:
            acc[:, cols] += _dot(a_ref[...], w_ref[:, cols])

    @pl.when(k == pl.num_programs(1) - 1)
    def _():
        def hbm_rows(ref, r):
            return ref.at[pl.ds(pl.multiple_of(i * tm + r * PROJ_ROWS, PROJ_ROWS), PROJ_ROWS), :]

        def x_copy(r, slot):
            return pltpu.make_async_copy(hbm_rows(x_hbm, r), x_buf.at[slot], x_sem.at[slot])

        def o_copy(r, slot):
            return pltpu.make_async_copy(o_buf.at[slot], hbm_rows(o_hbm, r), o_sem.at[slot])

        def ob_copy(r, slot):
            return pltpu.make_async_copy(ob_buf.at[slot], hbm_rows(ob_hbm, r), ob_sem.at[slot])

        def slab(r, slot):
            @pl.when(r + 1 < slabs)
            def _():
                x_copy(r + 1, 1 - slot).start()

            x_copy(r, slot).wait()

            @pl.when(r >= 2)
            def _():
                o_copy(r - 2, slot).wait()
                ob_copy(r - 2, slot).wait()

            rows = pl.ds(pl.multiple_of(r * PROJ_ROWS, PROJ_ROWS), PROJ_ROWS)
            y = _layer_norm(ALPHA * x_buf[slot] + acc[rows, :], g_ref[...], b_ref[...])
            o_buf[slot] = y
            ob_buf[slot] = y.astype(BF16)
            o_copy(r, slot).start()
            ob_copy(r, slot).start()

        x_copy(0, 0).start()

        def pair(p, carry):
            slab(2 * p, 0)
            slab(2 * p + 1, 1)
            return carry

        lax.fori_loop(0, slabs // 2, pair, 0)
        for r in (slabs - 2, slabs - 1):
            o_copy(r, r % 2).wait()
            ob_copy(r, r % 2).wait()


def _proj_ln_tall(a, w, layer, x, g, b, name, tm=1024, tk=512):
    n, d = x.shape
    kdim = a.shape[1]
    tm = min(tm, n)
    assert kdim % tk == 0 and d % PROJ_COLS == 0 and tm % (2 * PROJ_ROWS) == 0 and tm // PROJ_ROWS >= 2
    vec = lambda: pl.BlockSpec((1, d), lambda i, k: (0, 0))
    hbm = lambda: pl.BlockSpec(memory_space=pl.ANY)
    return pl.pallas_call(
        _proj_ln_tall_kernel,
        grid=(n // tm, kdim // tk),
        in_specs=[pl.BlockSpec((tm, tk), lambda i, k: (i, k)),
                  pl.BlockSpec((None, tk, d), lambda i, k: (layer, k, 0)),
                  hbm(), vec(), vec()],
        out_specs=[hbm(), hbm()],
        out_shape=[jax.ShapeDtypeStruct((n, d), F32), jax.ShapeDtypeStruct((n, d), BF16)],
        scratch_shapes=[pltpu.VMEM((tm, d), F32),
                        pltpu.VMEM((2, PROJ_ROWS, d), F32),
                        pltpu.VMEM((2, PROJ_ROWS, d), F32),
                        pltpu.VMEM((2, PROJ_ROWS, d), BF16),
                        pltpu.SemaphoreType.DMA((2,)),
                        pltpu.SemaphoreType.DMA((2,)),
                        pltpu.SemaphoreType.DMA((2,))],
        compiler_params=_params(("arbitrary", "arbitrary")),
        name=name,
    )(a, w, x, g.reshape(1, d), b.reshape(1, d))


def _ffn_up_kernel(x_ref, wg_ref, wu_ref, o_ref):
    last = pl.num_programs(1) - 1

    @pl.when(pl.program_id(1) < last)
    def _():
        x = x_ref[...]
        gate = _dot(x, wg_ref[...].astype(BF16))
        up = _dot(x, wu_ref[...].astype(BF16))
        o_ref[...] = (gate * _sigmoid(gate) * up).astype(o_ref.dtype)

    @pl.when(pl.program_id(1) == last)
    def _():
        o_ref[...] = jnp.zeros_like(o_ref)


def _ffn_up(xb, wg, wu, layer, tm=2048, tn=256):
    n, d = xb.shape
    f = wg.shape[2]
    assert f % tn == 0 and FFN_PAD - f == tn
    tm = min(tm, n)
    nj = f // tn
    wspec = lambda: pl.BlockSpec((None, d, tn), lambda i, j: (layer, 0, jnp.minimum(j, nj - 1)))
    return pl.pallas_call(
        _ffn_up_kernel,
        grid=(n // tm, nj + 1),
        in_specs=[pl.BlockSpec((tm, d), lambda i, j: (i, 0), pipeline_mode=pl.Buffered(1)), wspec(), wspec()],
        out_specs=pl.BlockSpec((tm, tn), lambda i, j: (i, j)),
        out_shape=jax.ShapeDtypeStruct((n, FFN_PAD), BF16),
        compiler_params=_params(("arbitrary", "arbitrary")),
        name="ffn_up",
    )(xb, wg, wu)


def kernel(x, positions, emb_ln_g, emb_ln_b, w_in, sgu_ln_g, sgu_ln_b, sgu_w, sgu_b, conv_dw_w, conv_dw_b, conv_norm_g, conv_norm_b, conv_pw_w, conv_pw_b, hgrn_lower_bound, hgrn_norm_g, attn_sinks, w_o, ln1_g, ln1_b, w_gate, w_up, w_down, ln2_g, ln2_b):
    bsz, seq, d = x.shape
    n = bsz * seq
    depth = w_in.shape[0]
    p_lb = jax.nn.softmax(hgrn_lower_bound.astype(F32), axis=0)
    lower_bounds = jnp.cumsum(p_lb, axis=0) - p_lb[0]
    wo_b = w_o.astype(BF16)
    wd_b = jnp.pad(w_down.astype(BF16), ((0, 0), (0, FFN_PAD - w_down.shape[1]), (0, 0)))
    pw_b = conv_pw_w.astype(BF16)

    xf, xb = _emb_ln(x.reshape(n, d), emb_ln_g, emb_ln_b)
    for l in range(depth):
        h = _in_proj(xb, w_in, l)
        y_a = _sgu(h, sgu_ln_g[l], sgu_ln_b[l], sgu_w[l], sgu_b[l])
        y_b = _conv(h, bsz, conv_dw_w[l], conv_dw_b[l], conv_norm_g[l], conv_norm_b[l], pw_b[l], conv_pw_b[l])
        y_c = _hgrn(h, bsz, lower_bounds[l], hgrn_norm_g[l])
        y_d = _swa(h, bsz, positions, attn_sinks[l])
        y = jnp.concatenate([y_a, y_b, y_c, y_d], axis=-1)
        xf, xb = _proj_ln(y, wo_b, l, xf, ln1_g[l], ln1_b[l], "out_proj_ln", tm=256, tk=d)
        hid = _ffn_up(xb, w_gate, w_up, l)
        xf, xb = _proj_ln_tall(hid, wd_b, l, xf, ln2_g[l], ln2_b[l], "ffn_down_ln", tk=1024)
    return xf.reshape(bsz, seq, d)
```

```python
import functools
import math

import numpy as np
import jax
import jax.numpy as jnp
from jax import lax
from jax.experimental import pallas as pl
from jax.experimental.pallas import tpu as pltpu

F32 = jnp.float32
BF16 = jnp.bfloat16

D_MODEL = 4096
DEPTH = 4
GW = D_MODEL // 4
LANES = 128
SGU_CHUNK = 128
SGU_HEADS = 8
CONV_WIDTH = 31
CONV_HALO = 32
CONV_ROWS = 128
HGRN_CHUNK = 128
HGRN_SUB = 8
HGRN_LEVELS = (64, 32, 16, 8)
HGRN_STEP_HEADS = 2
ATTN_HEAD_DIM = 64
ATTN_Q_HEADS = 16
ATTN_KV_HEADS = 2
WINDOW = 128
ROPE_DIM = 16
ROPE_THETA = 500000.0
FFN_HIDDEN = 11008
FFN_PAD = 11264
PROJ_COLS = 512
PROJ_ROWS = 64
IN_COLS = 2 * GW + 2 * GW + 4 * GW + (ATTN_Q_HEADS + 2 * ATTN_KV_HEADS) * ATTN_HEAD_DIM
ALPHA = (2 * DEPTH) ** 0.25
LN_EPS = 1e-5
VMEM_LIMIT = 60 * 1024 * 1024


def _params(sem):
    return pltpu.CompilerParams(dimension_semantics=sem, vmem_limit_bytes=VMEM_LIMIT)


def _sigmoid(x):
    return 1.0 / (1.0 + jnp.exp(-x))


def _gelu(x):
    return 0.5 * x * (1.0 + lax.erf(x * (1.0 / math.sqrt(2.0))))


def _layer_norm(x, g, b):
    mu = jnp.mean(x, axis=-1, keepdims=True)
    xc = x - mu
    var = jnp.mean(xc * xc, axis=-1, keepdims=True)
    return xc * lax.rsqrt(var + LN_EPS) * g + b


def _layer_norm_streamed(load, g_ref, b_ref, store, d):
    tiles = [slice(c, c + LANES) for c in range(0, d, LANES)]
    total = load(tiles[0])
    for cols in tiles[1:]:
        total = total + load(cols)
    mu = jnp.sum(total, axis=-1, keepdims=True) / d
    sq = jnp.square(load(tiles[0]) - mu)
    for cols in tiles[1:]:
        sq = sq + jnp.square(load(cols) - mu)
    rs = lax.rsqrt(jnp.sum(sq, axis=-1, keepdims=True) / d + LN_EPS)
    for cols in tiles:
        store(cols, (load(cols) - mu) * rs * g_ref[:, cols] + b_ref[:, cols])


def _dot(a, b):
    return jnp.dot(a, b, preferred_element_type=F32)


def _dot_nt(a, b):
    return lax.dot_general(a, b, (((1,), (1,)), ((), ())), preferred_element_type=F32)


def _dot_tn(a, b):
    return lax.dot_general(a, b, (((0,), (0,)), ((), ())), preferred_element_type=F32)


def _emb_ln_kernel(x_ref, g_ref, b_ref, o_ref, ob_ref):
    y = _layer_norm(x_ref[...], g_ref[...], b_ref[...])
    o_ref[...] = y
    ob_ref[...] = y.astype(BF16)


def _emb_ln(x, g, b, tm=256):
    n, d = x.shape
    return pl.pallas_call(
        _emb_ln_kernel,
        grid=(n // tm,),
        in_specs=[pl.BlockSpec((tm, d), lambda i: (i, 0)),
                  pl.BlockSpec((1, d), lambda i: (0, 0)),
                  pl.BlockSpec((1, d), lambda i: (0, 0))],
        out_specs=[pl.BlockSpec((tm, d), lambda i: (i, 0)),
                   pl.BlockSpec((tm, d), lambda i: (i, 0))],
        out_shape=[jax.ShapeDtypeStruct((n, d), F32), jax.ShapeDtypeStruct((n, d), BF16)],
        compiler_params=_params(("arbitrary",)),
        name="emb_ln",
    )(x, g.reshape(1, d), b.reshape(1, d))


def _in_proj_kernel(a_ref, w_ref, o_ref):
    o_ref[...] = _dot(a_ref[...], w_ref[...].astype(BF16))


def _in_proj(xb, w, layer, tm=2048, tn=512):
    n, d = xb.shape
    cols = w.shape[2]
    tm = min(tm, n)
    return pl.pallas_call(
        _in_proj_kernel,
        grid=(n // tm, pl.cdiv(cols, tn)),
        in_specs=[pl.BlockSpec((tm, d), lambda i, j: (i, 0), pipeline_mode=pl.Buffered(1)),
                  pl.BlockSpec((None, d, tn), lambda i, j: (layer, 0, j))],
        out_specs=pl.BlockSpec((tm, tn), lambda i, j: (i, j)),
        out_shape=jax.ShapeDtypeStruct((n, cols), F32),
        compiler_params=_params(("arbitrary", "arbitrary")),
        name="in_proj",
    )(xb, w)


def _sgu_kernel(u_ref, v_ref, g_ref, b_ref, w_ref, bs_ref, o_ref, *, tt):
    row = lax.broadcasted_iota(jnp.int32, (SGU_CHUNK, SGU_CHUNK), 0)
    col = lax.broadcasted_iota(jnp.int32, (SGU_CHUNK, SGU_CHUNK), 1)
    causal = row >= col
    for c in range(tt // SGU_CHUNK):
        rows = slice(c * SGU_CHUNK, (c + 1) * SGU_CHUNK)
        u = _gelu(u_ref[rows, :])
        v = _layer_norm(_gelu(v_ref[rows, :]), g_ref[...], b_ref[...]).astype(BF16)
        for hd in range(SGU_HEADS):
            cols = slice(hd * LANES, (hd + 1) * LANES)
            w = jnp.where(causal, w_ref[hd], 0.0).astype(BF16)
            mixed = _dot(w, v[:, cols]) + bs_ref[:, hd:hd + 1]
            o_ref[rows, cols] = (u[:, cols] * mixed).astype(o_ref.dtype)


def _sgu(h, ln_g, ln_b, w_s, b_s, tt=256):
    n = h.shape[0]
    return pl.pallas_call(
        functools.partial(_sgu_kernel, tt=tt),
        grid=(n // tt,),
        in_specs=[pl.BlockSpec((tt, GW), lambda i: (i, 0)),
                  pl.BlockSpec((tt, GW), lambda i: (i, 1)),
                  pl.BlockSpec((1, GW), lambda i: (0, 0)),
                  pl.BlockSpec((1, GW), lambda i: (0, 0)),
                  pl.BlockSpec((SGU_HEADS, SGU_CHUNK, SGU_CHUNK), lambda i: (0, 0, 0)),
                  pl.BlockSpec((SGU_CHUNK, SGU_HEADS), lambda i: (0, 0))],
        out_specs=pl.BlockSpec((tt, GW), lambda i: (i, 0)),
        out_shape=jax.ShapeDtypeStruct((n, GW), BF16),
        compiler_params=_params(("arbitrary",)),
        name="sgu_mixer",
    )(h, h, ln_g.reshape(1, GW), ln_b.reshape(1, GW), w_s, b_s.T)


def _conv_kernel(a_ref, gate_ref, dww_ref, dwb_ref, ng_ref, nb_ref, pww_ref, pwb_ref, o_ref,
                 hbuf, act, *, tt):
    @pl.when(pl.program_id(1) == 0)
    def _():
        hbuf[:, 0:CONV_HALO, :] = jnp.zeros((GW // LANES, CONV_HALO, LANES), F32)

    first = CONV_HALO - (CONV_WIDTH - 1)
    for c in range(GW // LANES):
        cols = slice(c * LANES, (c + 1) * LANES)
        hbuf[c, CONV_HALO:CONV_HALO + tt, :] = a_ref[:, cols] * _sigmoid(gate_ref[:, cols])
        for r0 in range(0, tt, CONV_ROWS):
            acc = jnp.broadcast_to(dwb_ref[:, cols], (CONV_ROWS, LANES))
            for k in range(CONV_WIDTH):
                acc = acc + dww_ref[k:k + 1, cols] * hbuf[c, r0 + first + k:r0 + first + k + CONV_ROWS, :]
            mu = jnp.mean(acc, axis=-1, keepdims=True)
            xc = acc - mu
            var = jnp.mean(xc * xc, axis=-1, keepdims=True)
            hn = xc * lax.rsqrt(var + LN_EPS) * ng_ref[:, cols] + nb_ref[:, cols]
            act[r0:r0 + CONV_ROWS, cols] = (hn * _sigmoid(hn)).astype(BF16)
        hbuf[c, 0:CONV_HALO, :] = hbuf[c, tt:tt + CONV_HALO, :]
    o_ref[...] = (_dot(act[...], pww_ref[...]) + pwb_ref[...]).astype(o_ref.dtype)


def _conv(h, bsz, dw_w, dw_b, norm_g, norm_b, pw_w, pw_b, tt=256):
    n = h.shape[0]
    nt = n // bsz // tt
    row = lambda b, t: b * nt + t
    vec = lambda: pl.BlockSpec((1, GW), lambda b, t: (0, 0))
    return pl.pallas_call(
        functools.partial(_conv_kernel, tt=tt),
        grid=(bsz, nt),
        in_specs=[pl.BlockSpec((tt, GW), lambda b, t: (row(b, t), 2)),
                  pl.BlockSpec((tt, GW), lambda b, t: (row(b, t), 3)),
                  pl.BlockSpec((CONV_WIDTH, GW), lambda b, t: (0, 0)),
                  vec(), vec(), vec(),
                  pl.BlockSpec((GW, GW), lambda b, t: (0, 0)),
                  vec()],
        out_specs=pl.BlockSpec((tt, GW), lambda b, t: (row(b, t), 0)),
        out_shape=jax.ShapeDtypeStruct((n, GW), BF16),
        scratch_shapes=[pltpu.VMEM((GW // LANES, CONV_HALO + tt, LANES), F32), pltpu.VMEM((tt, GW), BF16)],
        compiler_params=_params(("arbitrary", "arbitrary")),
        name="conv_mixer",
    )(h, h, dw_w, dw_b.reshape(1, GW), norm_g.reshape(1, GW), norm_b.reshape(1, GW),
      pw_w, pw_b.reshape(1, GW))


def _hgrn_constants():
    t = np.arange(HGRN_CHUNK)[:, None]
    j = np.arange(HGRN_CHUNK)[None, :]
    tril = (t >= j).astype(np.float32)
    causal = t >= j
    levels = [causal & ((t ^ j) >= m) & ((t ^ j) < 2 * m) for m in HGRN_LEVELS]
    shifts = [causal & ((t // HGRN_SUB) == (j // HGRN_SUB)) & (t - j == s) for s in range(HGRN_SUB)]
    masks = np.stack(levels + shifts).astype(np.float32)
    return jnp.asarray(tril, BF16), jnp.asarray(masks, BF16)


def _hgrn_local(zq, zf, zi, lb, tril, masks_ref, ones, bbuf, kbuf, pbuf):
    c = HGRN_CHUNK
    q = zq * _sigmoid(zq)
    f = lb + (1.0 - lb) * _sigmoid(zf)
    lf = jnp.log(f) * (1.0 / math.log(2.0))
    k = 1.0 - f
    v = zi.astype(BF16)
    hi = lf.astype(BF16)
    r1 = lf - hi.astype(F32)
    mid = r1.astype(BF16)
    lo = (r1 - mid.astype(F32)).astype(BF16)
    b = _dot(tril, hi) + _dot(tril, mid) + _dot(tril, lo)
    bbuf[HGRN_SUB:HGRN_SUB + c, :] = b
    kbuf[HGRN_SUB:HGRN_SUB + c, :] = k

    q_dec = (q * jnp.exp2(b)).astype(BF16)

    parts = []
    for m in HGRN_LEVELS:
        refs = [jnp.broadcast_to(bbuf[HGRN_SUB + g * 2 * m + m - 1:HGRN_SUB + g * 2 * m + m, :], (2 * m, LANES))
                for g in range(c // (2 * m))]
        ref = refs[0] if len(refs) == 1 else jnp.concatenate(refs, axis=0)
        e = jnp.exp2(-jnp.abs(b - ref))
        parts.append(_dot_nt((q * e).astype(BF16), (k * e).astype(BF16)))

    for s in range(HGRN_SUB):
        bs = bbuf[HGRN_SUB - s:HGRN_SUB - s + c, :]
        ks = kbuf[HGRN_SUB - s:HGRN_SUB - s + c, :]
        p = q * ks * jnp.exp2(jnp.minimum(b - bs, 0.0))
        pbuf[s * c:(s + 1) * c, :] = p.astype(BF16)
    near = _dot(pbuf[...], ones)
    parts += [near[s * c:(s + 1) * c, :] for s in range(HGRN_SUB)]

    scores = masks_ref[0] * parts[0].astype(BF16)
    for i in range(1, len(parts)):
        scores = scores + masks_ref[i] * parts[i].astype(BF16)
    o_intra = _dot(scores, v)
    b_end = b[c - 1:c, :]
    k_dec = (k * jnp.exp2(b_end - b)).astype(BF16)
    return q_dec, o_intra, k_dec, v, jnp.exp2(b_end)


def _hgrn_kernel(zq_ref, zf_ref, zi_ref, zg_ref, lb_ref, ng_ref, tril_ref, masks_ref, o_ref,
                 state, bbuf, kbuf, pbuf, *, tt):
    chunks = tt // HGRN_CHUNK

    @pl.when(pl.program_id(2) == 0)
    def _():
        state[...] = jnp.zeros_like(state)
        bbuf[:, 0:HGRN_SUB, :] = jnp.zeros((bbuf.shape[0], HGRN_SUB, LANES), F32)
        kbuf[:, 0:HGRN_SUB, :] = jnp.zeros((kbuf.shape[0], HGRN_SUB, LANES), F32)

    def window(ci, hh):
        return slice(ci * HGRN_CHUNK, (ci + 1) * HGRN_CHUNK), slice(hh * LANES, (hh + 1) * LANES)

    ones = jnp.ones((LANES, LANES), BF16)
    local = {}
    for ci in range(chunks):
        for hh in range(HGRN_STEP_HEADS):
            rows, cols = window(ci, hh)
            slot = ci * HGRN_STEP_HEADS + hh
            local[ci, hh] = _hgrn_local(zq_ref[rows, cols], zf_ref[rows, cols], zi_ref[rows, cols],
                                        lb_ref[0, :, cols], tril_ref[...], masks_ref, ones,
                                        bbuf.at[slot], kbuf.at[slot], pbuf.at[slot])
    for hh in range(HGRN_STEP_HEADS):
        st = state[hh]
        for ci in range(chunks):
            rows, cols = window(ci, hh)
            q_dec, o_intra, k_dec, v, decay = local[ci, hh]
            o = _dot_nt(q_dec, st.astype(BF16)) + o_intra
            st = decay * st + _dot_tn(v, k_dec)
            o = o * lax.rsqrt(jnp.mean(o * o, axis=-1, keepdims=True) + LN_EPS)
            o_ref[rows, cols] = (o * ng_ref[0, :, cols] * _sigmoid(zg_ref[rows, cols])).astype(o_ref.dtype)
        state[hh] = st


def _hgrn(h, bsz, lower_bound, norm_g, tt=512):
    n = h.shape[0]
    nt = n // bsz // tt
    tril, masks = _hgrn_constants()
    width = HGRN_STEP_HEADS * LANES
    slots = HGRN_STEP_HEADS * (tt // HGRN_CHUNK)
    groups = GW // width
    base = 4 * GW // width

    def zspec(part):
        return pl.BlockSpec((tt, width), lambda b, hg, t: (b * nt + t, base + part * groups + hg))

    vec = lambda: pl.BlockSpec((1, 1, width), lambda b, hg, t: (hg, 0, 0))
    return pl.pallas_call(
        functools.partial(_hgrn_kernel, tt=tt),
        grid=(bsz, groups, nt),
        in_specs=[zspec(0), zspec(1), zspec(2), zspec(3), vec(), vec(),
                  pl.BlockSpec((HGRN_CHUNK, HGRN_CHUNK), lambda b, hg, t: (0, 0)),
                  pl.BlockSpec(masks.shape, lambda b, hg, t: (0, 0, 0))],
        out_specs=pl.BlockSpec((tt, width), lambda b, hg, t: (b * nt + t, hg)),
        out_shape=jax.ShapeDtypeStruct((n, GW), BF16),
        scratch_shapes=[pltpu.VMEM((HGRN_STEP_HEADS, LANES, LANES), F32),
                        pltpu.VMEM((slots, HGRN_SUB + HGRN_CHUNK, LANES), F32),
                        pltpu.VMEM((slots, HGRN_SUB + HGRN_CHUNK, LANES), F32),
                        pltpu.VMEM((slots, HGRN_SUB * HGRN_CHUNK, LANES), BF16)],
        compiler_params=_params(("arbitrary", "arbitrary", "arbitrary")),
        name="hgrn_mixer",
    )(h, h, h, h, lower_bound.reshape(groups, 1, width), norm_g.reshape(groups, 1, width), tril, masks)


def _rope_constants():
    lane = np.arange(LANES) % ATTN_HEAD_DIM
    half = ROPE_DIM // 2
    inv_freq = ROPE_THETA ** (-jnp.arange(0, ROPE_DIM, 2, dtype=F32) / ROPE_DIM)
    freq = jnp.where(lane < ROPE_DIM, inv_freq[lane % half], 0.0).astype(F32)
    rot = (lane < ROPE_DIM).astype(np.float32)
    up = ((lane >= half) & (lane < ROPE_DIM)).astype(np.float32)
    dn = (lane < half).astype(np.float32)
    src = np.arange(LANES)[:, None]
    dst = np.arange(LANES)[None, :]
    perm = ((up[None, :] > 0) & (src == dst - half)) | ((dn[None, :] > 0) & (src == dst + half))
    consts = jnp.stack([freq, jnp.asarray(rot), jnp.asarray(up), jnp.asarray(dn)])[:, None, :]
    return consts, jnp.asarray(perm.astype(np.float32), BF16)


def _swa_kernel(sink_ref, pos_ref, q_ref, k_ref, v_ref, rc_ref, perm_ref, o_ref, kprev, vprev):
    t = pl.program_id(1)
    w = WINDOW

    @pl.when(t == 0)
    def _():
        kprev[...] = jnp.zeros_like(kprev)
        vprev[...] = jnp.zeros_like(vprev)

    ang = pos_ref[...].astype(F32) * rc_ref[0]
    c_mul = jnp.cos(ang) * rc_ref[1] + (1.0 - rc_ref[1])
    s_mul = jnp.sin(ang) * (rc_ref[2] - rc_ref[3])
    tiles = GW // LANES

    x = jnp.concatenate([q_ref[:, c * LANES:(c + 1) * LANES] for c in range(tiles)] + [k_ref[...]], axis=0)
    hi = x.astype(BF16)
    lo = (x - hi.astype(F32)).astype(BF16)
    partner = _dot(hi, perm_ref[...]) + _dot(lo, perm_ref[...])
    roped = (x.reshape(tiles + 1, w, LANES) * c_mul + partner.reshape(tiles + 1, w, LANES) * s_mul)

    lane = lax.broadcasted_iota(jnp.int32, (w, LANES), 1)
    low = lane < ATTN_HEAD_DIM
    k_cur = roped[tiles]
    v_cur = v_ref[...]
    k_sw = pltpu.roll(k_cur, ATTN_HEAD_DIM, 1)
    v_sw = pltpu.roll(v_cur, ATTN_HEAD_DIM, 1)
    k_dup = [jnp.where(low, k_cur, k_sw).astype(BF16), jnp.where(low, k_sw, k_cur).astype(BF16)]
    v_dup = [jnp.where(low, v_cur, v_sw).astype(BF16), jnp.where(low, v_sw, v_cur).astype(BF16)]

    qi = lax.broadcasted_iota(jnp.int32, (w, w), 0)
    kj = lax.broadcasted_iota(jnp.int32, (w, w), 1)
    from_prev = kj > qi
    no_prev = jnp.where(t > 0, 0.0, -jnp.inf)
    scale = ATTN_HEAD_DIM ** -0.5
    group = ATTN_Q_HEADS // ATTN_KV_HEADS
    ones = jnp.ones((w, LANES), BF16)

    for kv in range(ATTN_KV_HEADS):
        qs = []
        for pair in range(kv * group // 2, (kv + 1) * group // 2):
            qt = roped[pair] * scale
            qs += [jnp.where(low, qt, 0.0).astype(BF16), jnp.where(low, 0.0, qt).astype(BF16)]
        qst = jnp.concatenate(qs, axis=0)
        s_prev = _dot_nt(qst, kprev[kv]).reshape(group, w, w)
        s_cur = _dot_nt(qst, k_dup[kv]).reshape(group, w, w)
        s = jnp.where(from_prev, s_prev + no_prev, s_cur)
        sink = sink_ref[kv * group:(kv + 1) * group]
        m = jnp.maximum(jnp.max(s, axis=-1, keepdims=True), sink)
        p = jnp.exp(s - m)
        p_prev = jnp.where(from_prev, p, 0.0).astype(BF16).reshape(group * w, w)
        p_cur = jnp.where(from_prev, 0.0, p).astype(BF16).reshape(group * w, w)
        total = (_dot(p_prev, ones) + _dot(p_cur, ones)).reshape(group, w, LANES)
        acc = (_dot(p_prev, vprev[kv]) + _dot(p_cur, v_dup[kv])).reshape(group, w, LANES)
        acc = acc / (total + jnp.exp(sink - m))
        for g in range(0, group, 2):
            pair = (kv * group + g) // 2
            o_ref[:, pair * LANES:(pair + 1) * LANES] = jnp.where(low, acc[g], acc[g + 1]).astype(o_ref.dtype)

    for kv in range(ATTN_KV_HEADS):
        kprev[kv] = k_dup[kv]
        vprev[kv] = v_dup[kv]


def _swa(h, bsz, positions, sinks):
    n = h.shape[0]
    nt = n // bsz // WINDOW
    row = lambda b, t: b * nt + t
    qblk = 8
    kblk = (9 * GW) // LANES
    return pl.pallas_call(
        _swa_kernel,
        grid=(bsz, nt),
        in_specs=[pl.BlockSpec((ATTN_Q_HEADS, 1, 1), lambda b, t: (0, 0, 0)),
                  pl.BlockSpec((WINDOW, 1), lambda b, t: (row(b, t), 0)),
                  pl.BlockSpec((WINDOW, GW), lambda b, t: (row(b, t), qblk)),
                  pl.BlockSpec((WINDOW, LANES), lambda b, t: (row(b, t), kblk)),
                  pl.BlockSpec((WINDOW, LANES), lambda b, t: (row(b, t), kblk + 1)),
                  pl.BlockSpec((4, 1, LANES), lambda b, t: (0, 0, 0)),
                  pl.BlockSpec((LANES, LANES), lambda b, t: (0, 0))],
        out_specs=pl.BlockSpec((WINDOW, GW), lambda b, t: (row(b, t), 0)),
        out_shape=jax.ShapeDtypeStruct((n, GW), BF16),
        scratch_shapes=[pltpu.VMEM((ATTN_KV_HEADS, WINDOW, LANES), BF16),
                        pltpu.VMEM((ATTN_KV_HEADS, WINDOW, LANES), BF16)],
        compiler_params=_params(("arbitrary", "arbitrary")),
        name="swa_mixer",
    )(sinks.astype(F32).reshape(ATTN_Q_HEADS, 1, 1), positions.reshape(n, 1), h, h, h, *_rope_constants())


def _out_proj_kernel(*refs):
    w_ref, x_ref, g_ref, b_ref, o_ref, ob_ref = refs[-6:]
    a_refs = refs[:-6]
    tm, d = o_ref.shape
    kp = w_ref.shape[0] // len(a_refs)

    for c in range(0, d, PROJ_COLS):
        cols = slice(c, c + PROJ_COLS)
        acc = ALPHA * x_ref[:, cols]
        for p, a_ref in enumerate(a_refs):
            acc = acc + _dot(a_ref[...], w_ref[p * kp:(p + 1) * kp, cols])
        o_ref[:, cols] = acc

    def slab(r, carry):
        rows = pl.ds(pl.multiple_of(r * PROJ_ROWS, PROJ_ROWS), PROJ_ROWS)

        def store(cols, y):
            o_ref[rows, cols] = y
            ob_ref[rows, cols] = y.astype(BF16)

        _layer_norm_streamed(lambda cols: o_ref[rows, cols], g_ref, b_ref, store, d)
        return carry

    lax.fori_loop(0, tm // PROJ_ROWS, slab, 0)


def _out_proj(parts, w, layer, x, g, b, tm=256):
    n, d = x.shape
    kp = parts[0].shape[1]
    tm = min(tm, n)
    assert w.shape[1] == kp * len(parts) and d % PROJ_COLS == 0 and tm % PROJ_ROWS == 0
    row = lambda: pl.BlockSpec((tm, d), lambda i: (i, 0))
    vec = lambda: pl.BlockSpec((1, d), lambda i: (0, 0))
    return pl.pallas_call(
        _out_proj_kernel,
        grid=(n // tm,),
        in_specs=[pl.BlockSpec((tm, kp), lambda i: (i, 0)) for _ in parts] + [
            pl.BlockSpec((None, w.shape[1], d), lambda i: (layer, 0, 0), pipeline_mode=pl.Buffered(1)),
            row(), vec(), vec()],
        out_specs=[row(), row()],
        out_shape=[jax.ShapeDtypeStruct((n, d), F32), jax.ShapeDtypeStruct((n, d), BF16)],
        compiler_params=_params(("arbitrary",)),
        name="out_proj_ln",
    )(*parts, w, x, g.reshape(1, d), b.reshape(1, d))


def _proj_ln_tall_kernel(a_ref, w_ref, x_hbm, g_ref, b_ref, o_hbm, ob_hbm,
                         acc, x_buf, o_buf, ob_buf, x_sem, o_sem, ob_sem, *, tail_rows):
    i = pl.program_id(0)
    k = pl.program_id(1)
    last = pl.num_programs(1) - 1
    tm, d = acc.shape
    tk = w_ref.shape[0]
    col_chunks = [slice(c, c + PROJ_COLS) for c in range(0, d, PROJ_COLS)]
    slabs = tm // PROJ_ROWS

    @pl.when(k == 0)
    def _():
        for cols in col_chunks:
            acc[:, cols] = _dot(a_ref[...], w_ref[:, cols].astype(BF16))

    @pl.when((k > 0) & (k < last))
    def _():
        for cols in col_chunks:
            acc[:, cols] += _dot(a_ref[...], w_ref[:, cols].astype(BF16))

    @pl.when(k == last)
    def _():
        real = lax.broadcasted_iota(jnp.int32, (tk, PROJ_COLS), 0) < tail_rows
        for cols in col_chunks:
            acc[:, cols] += _dot(a_ref[...], jnp.where(real, w_ref[:, cols], 0.0).astype(BF16))

    @pl.when(k == last)
    def _():
        def hbm_rows(ref, r):
            return ref.at[pl.ds(pl.multiple_of(i * tm + r * PROJ_ROWS, PROJ_ROWS), PROJ_ROWS), :]

        def x_copy(r, slot):
            return pltpu.make_async_copy(hbm_rows(x_hbm, r), x_buf.at[slot], x_sem.at[slot])

        def o_copy(r, slot):
            return pltpu.make_async_copy(o_buf.at[slot], hbm_rows(o_hbm, r), o_sem.at[slot])

        def ob_copy(r, slot):
            return pltpu.make_async_copy(ob_buf.at[slot], hbm_rows(ob_hbm, r), ob_sem.at[slot])

        def slab(r, slot):
            @pl.when(r + 1 < slabs)
            def _():
                x_copy(r + 1, 1 - slot).start()

            x_copy(r, slot).wait()

            @pl.when(r >= 2)
            def _():
                o_copy(r - 2, slot).wait()
                ob_copy(r - 2, slot).wait()

            rows = pl.ds(pl.multiple_of(r * PROJ_ROWS, PROJ_ROWS), PROJ_ROWS)
            o_buf[slot] = ALPHA * x_buf[slot] + acc[rows, :]

            def store(cols, y):
                o_buf[slot, :, cols] = y
                ob_buf[slot, :, cols] = y.astype(BF16)

            _layer_norm_streamed(lambda cols: o_buf[slot, :, cols], g_ref, b_ref, store, d)
            o_copy(r, slot).start()
            ob_copy(r, slot).start()

        x_copy(0, 0).start()

        def pair(p, carry):
            slab(2 * p, 0)
            slab(2 * p + 1, 1)
            return carry

        lax.fori_loop(0, slabs // 2, pair, 0)
        for r in (slabs - 2, slabs - 1):
            o_copy(r, r % 2).wait()
            ob_copy(r, r % 2).wait()


def _proj_ln_tall(a, w, layer, x, g, b, name, tm=1024, tk=512):
    n, d = x.shape
    kdim = a.shape[1]
    tm = min(tm, n)
    nk = kdim // tk
    tail_rows = w.shape[1] - (nk - 1) * tk
    assert kdim % tk == 0 and nk >= 2 and 0 < tail_rows <= tk
    assert d % PROJ_COLS == 0 and tm % (2 * PROJ_ROWS) == 0
    vec = lambda: pl.BlockSpec((1, d), lambda i, k: (0, 0))
    hbm = lambda: pl.BlockSpec(memory_space=pl.ANY)
    return pl.pallas_call(
        functools.partial(_proj_ln_tall_kernel, tail_rows=tail_rows),
        grid=(n // tm, nk),
        in_specs=[pl.BlockSpec((tm, tk), lambda i, k: (i, k)),
                  pl.BlockSpec((None, tk, d), lambda i, k: (layer, k, 0)),
                  hbm(), vec(), vec()],
        out_specs=[hbm(), hbm()],
        out_shape=[jax.ShapeDtypeStruct((n, d), F32), jax.ShapeDtypeStruct((n, d), BF16)],
        scratch_shapes=[pltpu.VMEM((tm, d), F32),
                        pltpu.VMEM((2, PROJ_ROWS, d), F32),
                        pltpu.VMEM((2, PROJ_ROWS, d), F32),
                        pltpu.VMEM((2, PROJ_ROWS, d), BF16),
                        pltpu.SemaphoreType.DMA((2,)),
                        pltpu.SemaphoreType.DMA((2,)),
                        pltpu.SemaphoreType.DMA((2,))],
        compiler_params=_params(("arbitrary", "arbitrary")),
        name=name,
    )(a, w, x, g.reshape(1, d), b.reshape(1, d))


def _ffn_up_kernel(x_ref, wg_ref, wu_ref, o_ref):
    last = pl.num_programs(1) - 1

    @pl.when(pl.program_id(1) < last)
    def _():
        x = x_ref[...]
        gate = _dot(x, wg_ref[...].astype(BF16))
        up = _dot(x, wu_ref[...].astype(BF16))
        o_ref[...] = (gate * _sigmoid(gate) * up).astype(o_ref.dtype)

    @pl.when(pl.program_id(1) == last)
    def _():
        o_ref[...] = jnp.zeros_like(o_ref)


def _ffn_up(xb, wg, wu, layer, tm=2048, tn=256):
    n, d = xb.shape
    f = wg.shape[2]
    assert f % tn == 0 and FFN_PAD - f == tn
    tm = min(tm, n)
    nj = f // tn
    wspec = lambda: pl.BlockSpec((None, d, tn), lambda i, j: (layer, 0, jnp.minimum(j, nj - 1)))
    return pl.pallas_call(
        _ffn_up_kernel,
        grid=(n // tm, nj + 1),
        in_specs=[pl.BlockSpec((tm, d), lambda i, j: (i, 0), pipeline_mode=pl.Buffered(1)), wspec(), wspec()],
        out_specs=pl.BlockSpec((tm, tn), lambda i, j: (i, j)),
        out_shape=jax.ShapeDtypeStruct((n, FFN_PAD), BF16),
        compiler_params=_params(("arbitrary", "arbitrary")),
        name="ffn_up",
    )(xb, wg, wu)


def kernel(x, positions, emb_ln_g, emb_ln_b, w_in, sgu_ln_g, sgu_ln_b, sgu_w, sgu_b, conv_dw_w, conv_dw_b, conv_norm_g, conv_norm_b, conv_pw_w, conv_pw_b, hgrn_lower_bound, hgrn_norm_g, attn_sinks, w_o, ln1_g, ln1_b, w_gate, w_up, w_down, ln2_g, ln2_b):
    bsz, seq, d = x.shape
    n = bsz * seq
    depth = w_in.shape[0]
    p_lb = jax.nn.softmax(hgrn_lower_bound.astype(F32), axis=0)
    lower_bounds = jnp.cumsum(p_lb, axis=0) - p_lb[0]
    wo_b = w_o.astype(BF16)
    pw_b = conv_pw_w.astype(BF16)

    xf, xb = _emb_ln(x.reshape(n, d), emb_ln_g, emb_ln_b)
    for l in range(depth):
        h = _in_proj(xb, w_in, l)
        y_a = _sgu(h, sgu_ln_g[l], sgu_ln_b[l], sgu_w[l], sgu_b[l])
        y_b = _conv(h, bsz, conv_dw_w[l], conv_dw_b[l], conv_norm_g[l], conv_norm_b[l], pw_b[l], conv_pw_b[l])
        y_c = _hgrn(h, bsz, lower_bounds[l], hgrn_norm_g[l])
        y_d = _swa(h, bsz, positions, attn_sinks[l])
        xf, xb = _out_proj((y_a, y_b, y_c, y_d), wo_b, l, xf, ln1_g[l], ln1_b[l])
        hid = _ffn_up(xb, w_gate, w_up, l)
        xf, xb = _proj_ln_tall(hid, w_down, l, xf, ln2_g[l], ln2_b[l], "ffn_down_ln")
    return xf.reshape(bsz, seq, d)
```

```python
import functools
import math

import numpy as np
import jax
import jax.numpy as jnp
from jax import lax
from jax.experimental import pallas as pl
from jax.experimental.pallas import tpu as pltpu

F32 = jnp.float32
BF16 = jnp.bfloat16

D_MODEL = 4096
DEPTH = 4
GW = D_MODEL // 4
LANES = 128
SGU_CHUNK = 128
SGU_HEADS = 8
CONV_WIDTH = 31
CONV_HALO = 32
CONV_ROWS = 128
HGRN_CHUNK = 128
HGRN_SUB = 8
HGRN_LEVELS = (64, 32, 16, 8)
HGRN_STEP_HEADS = 2
ATTN_HEAD_DIM = 64
ATTN_Q_HEADS = 16
ATTN_KV_HEADS = 2
WINDOW = 128
ROPE_DIM = 16
ROPE_THETA = 500000.0
FFN_HIDDEN = 11008
FFN_PAD = 11264
PROJ_COLS = 512
PROJ_ROWS = 64
IN_COLS = 2 * GW + 2 * GW + 4 * GW + (ATTN_Q_HEADS + 2 * ATTN_KV_HEADS) * ATTN_HEAD_DIM
ALPHA = (2 * DEPTH) ** 0.25
LN_EPS = 1e-5
VMEM_LIMIT = 60 * 1024 * 1024


def _params(sem):
    return pltpu.CompilerParams(dimension_semantics=sem, vmem_limit_bytes=VMEM_LIMIT)


def _sigmoid(x):
    return 1.0 / (1.0 + jnp.exp(-x))


def _gelu(x):
    return 0.5 * x * (1.0 + lax.erf(x * (1.0 / math.sqrt(2.0))))


def _layer_norm(x, g, b):
    mu = jnp.mean(x, axis=-1, keepdims=True)
    xc = x - mu
    var = jnp.mean(xc * xc, axis=-1, keepdims=True)
    return xc * lax.rsqrt(var + LN_EPS) * g + b


def _layer_norm_streamed(load, g_ref, b_ref, store, d):
    tiles = [slice(c, c + LANES) for c in range(0, d, LANES)]
    total = load(tiles[0])
    for cols in tiles[1:]:
        total = total + load(cols)
    mu = jnp.sum(total, axis=-1, keepdims=True) / d
    sq = jnp.square(load(tiles[0]) - mu)
    for cols in tiles[1:]:
        sq = sq + jnp.square(load(cols) - mu)
    rs = lax.rsqrt(jnp.sum(sq, axis=-1, keepdims=True) / d + LN_EPS)
    for cols in tiles:
        store(cols, (load(cols) - mu) * rs * g_ref[:, cols] + b_ref[:, cols])


def _dot(a, b):
    return jnp.dot(a, b, preferred_element_type=F32)


def _dot_nt(a, b):
    return lax.dot_general(a, b, (((1,), (1,)), ((), ())), preferred_element_type=F32)


def _dot_tn(a, b):
    return lax.dot_general(a, b, (((0,), (0,)), ((), ())), preferred_element_type=F32)


def _emb_ln_kernel(x_ref, g_ref, b_ref, o_ref, ob_ref):
    y = _layer_norm(x_ref[...], g_ref[...], b_ref[...])
    o_ref[...] = y
    ob_ref[...] = y.astype(BF16)


def _emb_ln(x, g, b, tm=256):
    n, d = x.shape
    return pl.pallas_call(
        _emb_ln_kernel,
        grid=(n // tm,),
        in_specs=[pl.BlockSpec((tm, d), lambda i: (i, 0)),
                  pl.BlockSpec((1, d), lambda i: (0, 0)),
                  pl.BlockSpec((1, d), lambda i: (0, 0))],
        out_specs=[pl.BlockSpec((tm, d), lambda i: (i, 0)),
                   pl.BlockSpec((tm, d), lambda i: (i, 0))],
        out_shape=[jax.ShapeDtypeStruct((n, d), F32), jax.ShapeDtypeStruct((n, d), BF16)],
        compiler_params=_params(("arbitrary",)),
        name="emb_ln",
    )(x, g.reshape(1, d), b.reshape(1, d))


def _in_proj_kernel(a_ref, w_ref, o_ref):
    o_ref[...] = _dot(a_ref[...], w_ref[...].astype(BF16))


def _in_proj(xb, w, layer, tm=2048, tn=512):
    n, d = xb.shape
    cols = w.shape[2]
    tm = min(tm, n)
    return pl.pallas_call(
        _in_proj_kernel,
        grid=(n // tm, pl.cdiv(cols, tn)),
        in_specs=[pl.BlockSpec((tm, d), lambda i, j: (i, 0), pipeline_mode=pl.Buffered(1)),
                  pl.BlockSpec((None, d, tn), lambda i, j: (layer, 0, j))],
        out_specs=pl.BlockSpec((tm, tn), lambda i, j: (i, j)),
        out_shape=jax.ShapeDtypeStruct((n, cols), F32),
        compiler_params=_params(("arbitrary", "arbitrary")),
        name="in_proj",
    )(xb, w)


def _sgu_kernel(u_ref, v_ref, g_ref, b_ref, w_ref, bs_ref, o_ref, *, tt):
    row = lax.broadcasted_iota(jnp.int32, (SGU_CHUNK, SGU_CHUNK), 0)
    col = lax.broadcasted_iota(jnp.int32, (SGU_CHUNK, SGU_CHUNK), 1)
    causal = row >= col
    for c in range(tt // SGU_CHUNK):
        rows = slice(c * SGU_CHUNK, (c + 1) * SGU_CHUNK)
        u = _gelu(u_ref[rows, :])
        v = _layer_norm(_gelu(v_ref[rows, :]), g_ref[...], b_ref[...]).astype(BF16)
        for hd in range(SGU_HEADS):
            cols = slice(hd * LANES, (hd + 1) * LANES)
            w = jnp.where(causal, w_ref[hd], 0.0).astype(BF16)
            mixed = _dot(w, v[:, cols]) + bs_ref[:, hd:hd + 1]
            o_ref[rows, cols] = (u[:, cols] * mixed).astype(o_ref.dtype)


def _sgu(h, ln_g, ln_b, w_s, b_s, tt=256):
    n = h.shape[0]
    return pl.pallas_call(
        functools.partial(_sgu_kernel, tt=tt),
        grid=(n // tt,),
        in_specs=[pl.BlockSpec((tt, GW), lambda i: (i, 0)),
                  pl.BlockSpec((tt, GW), lambda i: (i, 1)),
                  pl.BlockSpec((1, GW), lambda i: (0, 0)),
                  pl.BlockSpec((1, GW), lambda i: (0, 0)),
                  pl.BlockSpec((SGU_HEADS, SGU_CHUNK, SGU_CHUNK), lambda i: (0, 0, 0)),
                  pl.BlockSpec((SGU_CHUNK, SGU_HEADS), lambda i: (0, 0))],
        out_specs=pl.BlockSpec((tt, GW), lambda i: (i, 0)),
        out_shape=jax.ShapeDtypeStruct((n, GW), BF16),
        compiler_params=_params(("arbitrary",)),
        name="sgu_mixer",
    )(h, h, ln_g.reshape(1, GW), ln_b.reshape(1, GW), w_s, b_s.T)


def _conv_kernel(a_ref, gate_ref, dww_ref, dwb_ref, ng_ref, nb_ref, pww_ref, pwb_ref, o_ref,
                 hbuf, act, *, tt):
    @pl.when(pl.program_id(1) == 0)
    def _():
        hbuf[:, 0:CONV_HALO, :] = jnp.zeros((GW // LANES, CONV_HALO, LANES), F32)

    first = CONV_HALO - (CONV_WIDTH - 1)
    for c in range(GW // LANES):
        cols = slice(c * LANES, (c + 1) * LANES)
        hbuf[c, CONV_HALO:CONV_HALO + tt, :] = a_ref[:, cols] * _sigmoid(gate_ref[:, cols])
        for r0 in range(0, tt, CONV_ROWS):
            acc = jnp.broadcast_to(dwb_ref[:, cols], (CONV_ROWS, LANES))
            for k in range(CONV_WIDTH):
                acc = acc + dww_ref[k:k + 1, cols] * hbuf[c, r0 + first + k:r0 + first + k + CONV_ROWS, :]
            mu = jnp.mean(acc, axis=-1, keepdims=True)
            xc = acc - mu
            var = jnp.mean(xc * xc, axis=-1, keepdims=True)
            hn = xc * lax.rsqrt(var + LN_EPS) * ng_ref[:, cols] + nb_ref[:, cols]
            act[r0:r0 + CONV_ROWS, cols] = (hn * _sigmoid(hn)).astype(BF16)
        hbuf[c, 0:CONV_HALO, :] = hbuf[c, tt:tt + CONV_HALO, :]
    o_ref[...] = (_dot(act[...], pww_ref[...]) + pwb_ref[...]).astype(o_ref.dtype)


def _conv(h, bsz, dw_w, dw_b, norm_g, norm_b, pw_w, pw_b, tt=256):
    n = h.shape[0]
    nt = n // bsz // tt
    row = lambda b, t: b * nt + t
    vec = lambda: pl.BlockSpec((1, GW), lambda b, t: (0, 0))
    return pl.pallas_call(
        functools.partial(_conv_kernel, tt=tt),
        grid=(bsz, nt),
        in_specs=[pl.BlockSpec((tt, GW), lambda b, t: (row(b, t), 2)),
                  pl.BlockSpec((tt, GW), lambda b, t: (row(b, t), 3)),
                  pl.BlockSpec((CONV_WIDTH, GW), lambda b, t: (0, 0)),
                  vec(), vec(), vec(),
                  pl.BlockSpec((GW, GW), lambda b, t: (0, 0)),
                  vec()],
        out_specs=pl.BlockSpec((tt, GW), lambda b, t: (row(b, t), 0)),
        out_shape=jax.ShapeDtypeStruct((n, GW), BF16),
        scratch_shapes=[pltpu.VMEM((GW // LANES, CONV_HALO + tt, LANES), F32), pltpu.VMEM((tt, GW), BF16)],
        compiler_params=_params(("arbitrary", "arbitrary")),
        name="conv_mixer",
    )(h, h, dw_w, dw_b.reshape(1, GW), norm_g.reshape(1, GW), norm_b.reshape(1, GW),
      pw_w, pw_b.reshape(1, GW))


def _hgrn_constants():
    t = np.arange(HGRN_CHUNK)[:, None]
    j = np.arange(HGRN_CHUNK)[None, :]
    tril = (t >= j).astype(np.float32)
    causal = t >= j
    levels = [causal & ((t ^ j) >= m) & ((t ^ j) < 2 * m) for m in HGRN_LEVELS]
    shifts = [causal & ((t // HGRN_SUB) == (j // HGRN_SUB)) & (t - j == s) for s in range(HGRN_SUB)]
    which = np.full((HGRN_CHUNK, HGRN_CHUNK), -1, np.int32)
    for i, region in enumerate(levels + shifts):
        which[region] = i
    return jnp.asarray(tril, BF16), jnp.asarray(which)


def _hgrn_gates(zq, zf, lb, tril, qbuf, bbuf, kbuf):
    c = HGRN_CHUNK
    f = lb + (1.0 - lb) * _sigmoid(zf)
    lf = jnp.log(f) * (1.0 / math.log(2.0))
    hi = lf.astype(BF16)
    r1 = lf - hi.astype(F32)
    mid = r1.astype(BF16)
    lo = (r1 - mid.astype(F32)).astype(BF16)
    bbuf[HGRN_SUB:HGRN_SUB + c, :] = _dot(tril, hi) + _dot(tril, mid) + _dot(tril, lo)
    kbuf[HGRN_SUB:HGRN_SUB + c, :] = 1.0 - f
    qbuf[...] = zq * _sigmoid(zq)


def _hgrn_terms(ones, qbuf, bbuf, kbuf, terms):
    c = HGRN_CHUNK
    q = qbuf[...]
    b = bbuf[HGRN_SUB:HGRN_SUB + c, :]
    k = kbuf[HGRN_SUB:HGRN_SUB + c, :]

    for i, m in enumerate(HGRN_LEVELS):
        refs = [jnp.broadcast_to(bbuf[HGRN_SUB + g * 2 * m + m - 1:HGRN_SUB + g * 2 * m + m, :], (2 * m, LANES))
                for g in range(c // (2 * m))]
        ref = refs[0] if len(refs) == 1 else jnp.concatenate(refs, axis=0)
        e = jnp.exp2(-jnp.abs(b - ref))
        terms[i * c:(i + 1) * c, :] = _dot_nt((q * e).astype(BF16), (k * e).astype(BF16))

    shifted = []
    for s in range(HGRN_SUB):
        bs = bbuf[HGRN_SUB - s:HGRN_SUB - s + c, :]
        ks = kbuf[HGRN_SUB - s:HGRN_SUB - s + c, :]
        shifted.append((q * ks * jnp.exp2(jnp.minimum(b - bs, 0.0))).astype(BF16))
    terms[len(HGRN_LEVELS) * c:, :] = _dot(jnp.concatenate(shifted, axis=0), ones)

    b_end = b[c - 1:c, :]
    return (q * jnp.exp2(b)).astype(BF16), (k * jnp.exp2(b_end - b)).astype(BF16), jnp.exp2(b_end)


def _hgrn_intra(which_ref, terms, v):
    c = HGRN_CHUNK
    which = which_ref[...]
    scores = jnp.zeros((c, c), F32)
    for i in range(len(HGRN_LEVELS) + HGRN_SUB):
        scores = jnp.where(which == i, terms[i * c:(i + 1) * c, :], scores)
    return _dot(scores.astype(BF16), v)


def _hgrn_kernel(zq_ref, zf_ref, zi_ref, zg_ref, lb_ref, ng_ref, tril_ref, which_ref, o_ref,
                 state, qbuf, bbuf, kbuf, terms, *, tt):
    chunks = tt // HGRN_CHUNK

    @pl.when(pl.program_id(2) == 0)
    def _():
        state[...] = jnp.zeros_like(state)
        bbuf[:, 0:HGRN_SUB, :] = jnp.zeros((bbuf.shape[0], HGRN_SUB, LANES), F32)
        kbuf[:, 0:HGRN_SUB, :] = jnp.zeros((kbuf.shape[0], HGRN_SUB, LANES), F32)

    def window(ci, hh):
        return slice(ci * HGRN_CHUNK, (ci + 1) * HGRN_CHUNK), slice(hh * LANES, (hh + 1) * LANES)

    ones = jnp.ones((LANES, LANES), BF16)
    order = [(ci, hh) for ci in range(chunks) for hh in range(HGRN_STEP_HEADS)]
    for slot, (ci, hh) in enumerate(order):
        rows, cols = window(ci, hh)
        _hgrn_gates(zq_ref[rows, cols], zf_ref[rows, cols], lb_ref[0, :, cols], tril_ref[...],
                    qbuf.at[slot], bbuf.at[slot], kbuf.at[slot])

    def terms_of(slot):
        return _hgrn_terms(ones, qbuf.at[slot], bbuf.at[slot], kbuf.at[slot], terms.at[slot % 2])

    pending = terms_of(0)
    for slot, (ci, hh) in enumerate(order):
        q_dec, k_dec, decay = pending
        if slot + 1 < len(order):
            pending = terms_of(slot + 1)
        rows, cols = window(ci, hh)
        v = zi_ref[rows, cols].astype(BF16)
        st = state[hh]
        o = _dot_nt(q_dec, st.astype(BF16)) + _hgrn_intra(which_ref, terms.at[slot % 2], v)
        state[hh] = decay * st + _dot_tn(v, k_dec)
        o = o * lax.rsqrt(jnp.mean(o * o, axis=-1, keepdims=True) + LN_EPS)
        o_ref[rows, cols] = (o * ng_ref[0, :, cols] * _sigmoid(zg_ref[rows, cols])).astype(o_ref.dtype)


def _hgrn(h, bsz, lower_bound, norm_g, tt=512):
    n = h.shape[0]
    nt = n // bsz // tt
    tril, which = _hgrn_constants()
    width = HGRN_STEP_HEADS * LANES
    slots = HGRN_STEP_HEADS * (tt // HGRN_CHUNK)
    groups = GW // width
    base = 4 * GW // width

    def zspec(part):
        return pl.BlockSpec((tt, width), lambda b, hg, t: (b * nt + t, base + part * groups + hg))

    vec = lambda: pl.BlockSpec((1, 1, width), lambda b, hg, t: (hg, 0, 0))
    return pl.pallas_call(
        functools.partial(_hgrn_kernel, tt=tt),
        grid=(bsz, groups, nt),
        in_specs=[zspec(0), zspec(1), zspec(2), zspec(3), vec(), vec(),
                  pl.BlockSpec((HGRN_CHUNK, HGRN_CHUNK), lambda b, hg, t: (0, 0)),
                  pl.BlockSpec((HGRN_CHUNK, HGRN_CHUNK), lambda b, hg, t: (0, 0))],
        out_specs=pl.BlockSpec((tt, width), lambda b, hg, t: (b * nt + t, hg)),
        out_shape=jax.ShapeDtypeStruct((n, GW), BF16),
        scratch_shapes=[pltpu.VMEM((HGRN_STEP_HEADS, LANES, LANES), F32),
                        pltpu.VMEM((slots, HGRN_CHUNK, LANES), F32),
                        pltpu.VMEM((slots, HGRN_SUB + HGRN_CHUNK, LANES), F32),
                        pltpu.VMEM((slots, HGRN_SUB + HGRN_CHUNK, LANES), F32),
                        pltpu.VMEM((2, (len(HGRN_LEVELS) + HGRN_SUB) * HGRN_CHUNK, LANES), F32)],
        compiler_params=_params(("arbitrary", "arbitrary", "arbitrary")),
        name="hgrn_mixer",
    )(h, h, h, h, lower_bound.reshape(groups, 1, width), norm_g.reshape(groups, 1, width), tril, which)


def _rope_constants():
    lane = np.arange(LANES) % ATTN_HEAD_DIM
    half = ROPE_DIM // 2
    inv_freq = ROPE_THETA ** (-jnp.arange(0, ROPE_DIM, 2, dtype=F32) / ROPE_DIM)
    freq = jnp.where(lane < ROPE_DIM, inv_freq[lane % half], 0.0).astype(F32)
    rot = (lane < ROPE_DIM).astype(np.float32)
    up = ((lane >= half) & (lane < ROPE_DIM)).astype(np.float32)
    dn = (lane < half).astype(np.float32)
    src = np.arange(LANES)[:, None]
    dst = np.arange(LANES)[None, :]
    perm = ((up[None, :] > 0) & (src == dst - half)) | ((dn[None, :] > 0) & (src == dst + half))
    consts = jnp.stack([freq, jnp.asarray(rot), jnp.asarray(up), jnp.asarray(dn)])[:, None, :]
    return consts, jnp.asarray(perm.astype(np.float32), BF16)


def _swa_kernel(sink_ref, pos_ref, q_ref, k_ref, v_ref, rc_ref, perm_ref, o_ref, kprev, vprev):
    t = pl.program_id(1)
    w = WINDOW

    @pl.when(t == 0)
    def _():
        kprev[...] = jnp.zeros_like(kprev)
        vprev[...] = jnp.zeros_like(vprev)

    ang = pos_ref[...].astype(F32) * rc_ref[0]
    c_mul = jnp.cos(ang) * rc_ref[1] + (1.0 - rc_ref[1])
    s_mul = jnp.sin(ang) * (rc_ref[2] - rc_ref[3])
    tiles = GW // LANES

    x = jnp.concatenate([q_ref[:, c * LANES:(c + 1) * LANES] for c in range(tiles)] + [k_ref[...]], axis=0)
    hi = x.astype(BF16)
    lo = (x - hi.astype(F32)).astype(BF16)
    partner = _dot(hi, perm_ref[...]) + _dot(lo, perm_ref[...])
    roped = (x.reshape(tiles + 1, w, LANES) * c_mul + partner.reshape(tiles + 1, w, LANES) * s_mul)

    lane = lax.broadcasted_iota(jnp.int32, (w, LANES), 1)
    low = lane < ATTN_HEAD_DIM
    k_cur = roped[tiles]
    v_cur = v_ref[...]
    k_sw = pltpu.roll(k_cur, ATTN_HEAD_DIM, 1)
    v_sw = pltpu.roll(v_cur, ATTN_HEAD_DIM, 1)
    k_dup = [jnp.where(low, k_cur, k_sw).astype(BF16), jnp.where(low, k_sw, k_cur).astype(BF16)]
    v_dup = [jnp.where(low, v_cur, v_sw).astype(BF16), jnp.where(low, v_sw, v_cur).astype(BF16)]

    qi = lax.broadcasted_iota(jnp.int32, (w, w), 0)
    kj = lax.broadcasted_iota(jnp.int32, (w, w), 1)
    from_prev = kj > qi
    no_prev = jnp.where(t > 0, 0.0, -jnp.inf)
    scale = ATTN_HEAD_DIM ** -0.5
    group = ATTN_Q_HEADS // ATTN_KV_HEADS
    ones = jnp.ones((w, LANES), BF16)

    for kv in range(ATTN_KV_HEADS):
        qs = []
        for pair in range(kv * group // 2, (kv + 1) * group // 2):
            qt = roped[pair] * scale
            qs += [jnp.where(low, qt, 0.0).astype(BF16), jnp.where(low, 0.0, qt).astype(BF16)]
        qst = jnp.concatenate(qs, axis=0)
        s_prev = _dot_nt(qst, kprev[kv]).reshape(group, w, w)
        s_cur = _dot_nt(qst, k_dup[kv]).reshape(group, w, w)
        s = jnp.where(from_prev, s_prev + no_prev, s_cur)
        sink = sink_ref[kv * group:(kv + 1) * group]
        m = jnp.maximum(jnp.max(s, axis=-1, keepdims=True), sink)
        p = jnp.exp(s - m)
        p_prev = jnp.where(from_prev, p, 0.0).astype(BF16).reshape(group * w, w)
        p_cur = jnp.where(from_prev, 0.0, p).astype(BF16).reshape(group * w, w)
        total = (_dot(p_prev, ones) + _dot(p_cur, ones)).reshape(group, w, LANES)
        acc = (_dot(p_prev, vprev[kv]) + _dot(p_cur, v_dup[kv])).reshape(group, w, LANES)
        acc = acc / (total + jnp.exp(sink - m))
        for g in range(0, group, 2):
            pair = (kv * group + g) // 2
            o_ref[:, pair * LANES:(pair + 1) * LANES] = jnp.where(low, acc[g], acc[g + 1]).astype(o_ref.dtype)

    for kv in range(ATTN_KV_HEADS):
        kprev[kv] = k_dup[kv]
        vprev[kv] = v_dup[kv]


def _swa(h, bsz, positions, sinks):
    n = h.shape[0]
    nt = n // bsz // WINDOW
    row = lambda b, t: b * nt + t
    qblk = 8
    kblk = (9 * GW) // LANES
    return pl.pallas_call(
        _swa_kernel,
        grid=(bsz, nt),
        in_specs=[pl.BlockSpec((ATTN_Q_HEADS, 1, 1), lambda b, t: (0, 0, 0)),
                  pl.BlockSpec((WINDOW, 1), lambda b, t: (row(b, t), 0)),
                  pl.BlockSpec((WINDOW, GW), lambda b, t: (row(b, t), qblk)),
                  pl.BlockSpec((WINDOW, LANES), lambda b, t: (row(b, t), kblk)),
                  pl.BlockSpec((WINDOW, LANES), lambda b, t: (row(b, t), kblk + 1)),
                  pl.BlockSpec((4, 1, LANES), lambda b, t: (0, 0, 0)),
                  pl.BlockSpec((LANES, LANES), lambda b, t: (0, 0))],
        out_specs=pl.BlockSpec((WINDOW, GW), lambda b, t: (row(b, t), 0)),
        out_shape=jax.ShapeDtypeStruct((n, GW), BF16),
        scratch_shapes=[pltpu.VMEM((ATTN_KV_HEADS, WINDOW, LANES), BF16),
                        pltpu.VMEM((ATTN_KV_HEADS, WINDOW, LANES), BF16)],
        compiler_params=_params(("arbitrary", "arbitrary")),
        name="swa_mixer",
    )(sinks.astype(F32).reshape(ATTN_Q_HEADS, 1, 1), positions.reshape(n, 1), h, h, h, *_rope_constants())


def _out_proj_kernel(*refs):
    w_ref, x_ref, g_ref, b_ref, o_ref, ob_ref = refs[-6:]
    a_refs = refs[:-6]
    tm, d = o_ref.shape
    kp = w_ref.shape[0] // len(a_refs)

    for c in range(0, d, PROJ_COLS):
        cols = slice(c, c + PROJ_COLS)
        acc = ALPHA * x_ref[:, cols]
        for p, a_ref in enumerate(a_refs):
            acc = acc + _dot(a_ref[...], w_ref[p * kp:(p + 1) * kp, cols])
        o_ref[:, cols] = acc

    def slab(r, carry):
        rows = pl.ds(pl.multiple_of(r * PROJ_ROWS, PROJ_ROWS), PROJ_ROWS)

        def store(cols, y):
            o_ref[rows, cols] = y
            ob_ref[rows, cols] = y.astype(BF16)

        _layer_norm_streamed(lambda cols: o_ref[rows, cols], g_ref, b_ref, store, d)
        return carry

    lax.fori_loop(0, tm // PROJ_ROWS, slab, 0)


def _out_proj(parts, w, layer, x, g, b, tm=256):
    n, d = x.shape
    kp = parts[0].shape[1]
    tm = min(tm, n)
    assert w.shape[1] == kp * len(parts) and d % PROJ_COLS == 0 and tm % PROJ_ROWS == 0
    row = lambda: pl.BlockSpec((tm, d), lambda i: (i, 0))
    vec = lambda: pl.BlockSpec((1, d), lambda i: (0, 0))
    return pl.pallas_call(
        _out_proj_kernel,
        grid=(n // tm,),
        in_specs=[pl.BlockSpec((tm, kp), lambda i: (i, 0)) for _ in parts] + [
            pl.BlockSpec((None, w.shape[1], d), lambda i: (layer, 0, 0), pipeline_mode=pl.Buffered(1)),
            row(), vec(), vec()],
        out_specs=[row(), row()],
        out_shape=[jax.ShapeDtypeStruct((n, d), F32), jax.ShapeDtypeStruct((n, d), BF16)],
        compiler_params=_params(("arbitrary",)),
        name="out_proj_ln",
    )(*parts, w, x, g.reshape(1, d), b.reshape(1, d))


def _proj_ln_tall_kernel(a_ref, w_ref, x_hbm, g_ref, b_ref, o_hbm, ob_hbm,
                         acc, x_buf, o_buf, ob_buf, x_sem, o_sem, ob_sem, *, tail_rows):
    i = pl.program_id(0)
    k = pl.program_id(1)
    last = pl.num_programs(1) - 1
    tm, d = acc.shape
    tk = w_ref.shape[0]
    col_chunks = [slice(c, c + PROJ_COLS) for c in range(0, d, PROJ_COLS)]
    slabs = tm // PROJ_ROWS

    @pl.when(k == 0)
    def _():
        for cols in col_chunks:
            acc[:, cols] = _dot(a_ref[...], w_ref[:, cols].astype(BF16))

    @pl.when((k > 0) & (k < last))
    def _():
        for cols in col_chunks:
            acc[:, cols] += _dot(a_ref[...], w_ref[:, cols].astype(BF16))

    @pl.when(k == last)
    def _():
        real = lax.broadcasted_iota(jnp.int32, (tk, PROJ_COLS), 0) < tail_rows
        for cols in col_chunks:
            acc[:, cols] += _dot(a_ref[...], jnp.where(real, w_ref[:, cols], 0.0).astype(BF16))

    @pl.when(k == last)
    def _():
        def hbm_rows(ref, r):
            return ref.at[pl.ds(pl.multiple_of(i * tm + r * PROJ_ROWS, PROJ_ROWS), PROJ_ROWS), :]

        def x_copy(r, slot):
            return pltpu.make_async_copy(hbm_rows(x_hbm, r), x_buf.at[slot], x_sem.at[slot])

        def o_copy(r, slot):
            return pltpu.make_async_copy(o_buf.at[slot], hbm_rows(o_hbm, r), o_sem.at[slot])

        def ob_copy(r, slot):
            return pltpu.make_async_copy(ob_buf.at[slot], hbm_rows(ob_hbm, r), ob_sem.at[slot])

        def slab(r, slot):
            @pl.when(r + 1 < slabs)
            def _():
                x_copy(r + 1, 1 - slot).start()

            x_copy(r, slot).wait()

            @pl.when(r >= 2)
            def _():
                o_copy(r - 2, slot).wait()
                ob_copy(r - 2, slot).wait()

            rows = pl.ds(pl.multiple_of(r * PROJ_ROWS, PROJ_ROWS), PROJ_ROWS)
            o_buf[slot] = ALPHA * x_buf[slot] + acc[rows, :]

            def store(cols, y):
                o_buf[slot, :, cols] = y
                ob_buf[slot, :, cols] = y.astype(BF16)

            _layer_norm_streamed(lambda cols: o_buf[slot, :, cols], g_ref, b_ref, store, d)
            o_copy(r, slot).start()
            ob_copy(r, slot).start()

        x_copy(0, 0).start()

        def pair(p, carry):
            slab(2 * p, 0)
            slab(2 * p + 1, 1)
            return carry

        lax.fori_loop(0, slabs // 2, pair, 0)
        for r in (slabs - 2, slabs - 1):
            o_copy(r, r % 2).wait()
            ob_copy(r, r % 2).wait()


def _proj_ln_tall(a, w, layer, x, g, b, name, tm=1024, tk=512):
    n, d = x.shape
    kdim = a.shape[1]
    tm = min(tm, n)
    nk = kdim // tk
    tail_rows = w.shape[1] - (nk - 1) * tk
    assert kdim % tk == 0 and nk >= 2 and 0 < tail_rows <= tk
    assert d % PROJ_COLS == 0 and tm % (2 * PROJ_ROWS) == 0
    vec = lambda: pl.BlockSpec((1, d), lambda i, k: (0, 0))
    hbm = lambda: pl.BlockSpec(memory_space=pl.ANY)
    return pl.pallas_call(
        functools.partial(_proj_ln_tall_kernel, tail_rows=tail_rows),
        grid=(n // tm, nk),
        in_specs=[pl.BlockSpec((tm, tk), lambda i, k: (i, k)),
                  pl.BlockSpec((None, tk, d), lambda i, k: (layer, k, 0)),
                  hbm(), vec(), vec()],
        out_specs=[hbm(), hbm()],
        out_shape=[jax.ShapeDtypeStruct((n, d), F32), jax.ShapeDtypeStruct((n, d), BF16)],
        scratch_shapes=[pltpu.VMEM((tm, d), F32),
                        pltpu.VMEM((2, PROJ_ROWS, d), F32),
                        pltpu.VMEM((2, PROJ_ROWS, d), F32),
                        pltpu.VMEM((2, PROJ_ROWS, d), BF16),
                        pltpu.SemaphoreType.DMA((2,)),
                        pltpu.SemaphoreType.DMA((2,)),
                        pltpu.SemaphoreType.DMA((2,))],
        compiler_params=_params(("arbitrary", "arbitrary")),
        name=name,
    )(a, w, x, g.reshape(1, d), b.reshape(1, d))


def _ffn_up_kernel(x_ref, wg_ref, wu_ref, o_ref):
    last = pl.num_programs(1) - 1

    @pl.when(pl.program_id(1) < last)
    def _():
        x = x_ref[...]
        gate = _dot(x, wg_ref[...].astype(BF16))
        up = _dot(x, wu_ref[...].astype(BF16))
        o_ref[...] = (gate * _sigmoid(gate) * up).astype(o_ref.dtype)

    @pl.when(pl.program_id(1) == last)
    def _():
        o_ref[...] = jnp.zeros_like(o_ref)


def _ffn_up(xb, wg, wu, layer, tm=2048, tn=256):
    n, d = xb.shape
    f = wg.shape[2]
    assert f % tn == 0 and FFN_PAD - f == tn
    tm = min(tm, n)
    nj = f // tn
    wspec = lambda: pl.BlockSpec((None, d, tn), lambda i, j: (layer, 0, jnp.minimum(j, nj - 1)))
    return pl.pallas_call(
        _ffn_up_kernel,
        grid=(n // tm, nj + 1),
        in_specs=[pl.BlockSpec((tm, d), lambda i, j: (i, 0), pipeline_mode=pl.Buffered(1)), wspec(), wspec()],
        out_specs=pl.BlockSpec((tm, tn), lambda i, j: (i, j)),
        out_shape=jax.ShapeDtypeStruct((n, FFN_PAD), BF16),
        compiler_params=_params(("arbitrary", "arbitrary")),
        name="ffn_up",
    )(xb, wg, wu)


def kernel(x, positions, emb_ln_g, emb_ln_b, w_in, sgu_ln_g, sgu_ln_b, sgu_w, sgu_b, conv_dw_w, conv_dw_b, conv_norm_g, conv_norm_b, conv_pw_w, conv_pw_b, hgrn_lower_bound, hgrn_norm_g, attn_sinks, w_o, ln1_g, ln1_b, w_gate, w_up, w_down, ln2_g, ln2_b):
    bsz, seq, d = x.shape
    n = bsz * seq
    depth = w_in.shape[0]
    p_lb = jax.nn.softmax(hgrn_lower_bound.astype(F32), axis=0)
    lower_bounds = jnp.cumsum(p_lb, axis=0) - p_lb[0]
    wo_b = w_o.astype(BF16)
    pw_b = conv_pw_w.astype(BF16)

    xf, xb = _emb_ln(x.reshape(n, d), emb_ln_g, emb_ln_b)
    for l in range(depth):
        h = _in_proj(xb, w_in, l)
        y_a = _sgu(h, sgu_ln_g[l], sgu_ln_b[l], sgu_w[l], sgu_b[l])
        y_b = _conv(h, bsz, conv_dw_w[l], conv_dw_b[l], conv_norm_g[l], conv_norm_b[l], pw_b[l], conv_pw_b[l])
        y_c = _hgrn(h, bsz, lower_bounds[l], hgrn_norm_g[l])
        y_d = _swa(h, bsz, positions, attn_sinks[l])
        xf, xb = _out_proj((y_a, y_b, y_c, y_d), wo_b, l, xf, ln1_g[l], ln1_b[l])
        hid = _ffn_up(xb, w_gate, w_up, l)
        xf, xb = _proj_ln_tall(hid, w_down, l, xf, ln2_g[l], ln2_b[l], "ffn_down_ln")
    return xf.reshape(bsz, seq, d)
```

```python
import functools
import math

import numpy as np
import jax
import jax.numpy as jnp
from jax import lax
from jax.experimental import pallas as pl
from jax.experimental.pallas import tpu as pltpu

F32 = jnp.float32
BF16 = jnp.bfloat16

D_MODEL = 4096
DEPTH = 4
GW = D_MODEL // 4
LANES = 128
SGU_CHUNK = 128
SGU_HEADS = 8
CONV_WIDTH = 31
CONV_HALO = 32
CONV_ROWS = 128
HGRN_CHUNK = 128
HGRN_SUB = 8
HGRN_LEVELS = (64, 32, 16, 8)
HGRN_STEP_HEADS = 2
ATTN_HEAD_DIM = 64
ATTN_Q_HEADS = 16
ATTN_KV_HEADS = 2
WINDOW = 128
ROPE_DIM = 16
ROPE_THETA = 500000.0
FFN_PAD = 11264
PROJ_COLS = 512
PROJ_ROWS = 64
ALPHA = (2 * DEPTH) ** 0.25
LN_EPS = 1e-5
VMEM_LIMIT = 60 * 1024 * 1024


def _params(sem):
    return pltpu.CompilerParams(dimension_semantics=sem, vmem_limit_bytes=VMEM_LIMIT)


def _sigmoid(x):
    return 1.0 / (1.0 + jnp.exp(-x))


def _gelu(x):
    return 0.5 * x * (1.0 + lax.erf(x * (1.0 / math.sqrt(2.0))))


def _layer_norm(x, g, b):
    mu = jnp.mean(x, axis=-1, keepdims=True)
    xc = x - mu
    var = jnp.mean(xc * xc, axis=-1, keepdims=True)
    return xc * lax.rsqrt(var + LN_EPS) * g + b


def _layer_norm_streamed(load, g_ref, b_ref, store, d):
    tiles = [slice(c, c + LANES) for c in range(0, d, LANES)]
    total = load(tiles[0])
    for cols in tiles[1:]:
        total = total + load(cols)
    mu = jnp.sum(total, axis=-1, keepdims=True) / d
    sq = jnp.square(load(tiles[0]) - mu)
    for cols in tiles[1:]:
        sq = sq + jnp.square(load(cols) - mu)
    rs = lax.rsqrt(jnp.sum(sq, axis=-1, keepdims=True) / d + LN_EPS)
    for cols in tiles:
        store(cols, (load(cols) - mu) * rs * g_ref[:, cols] + b_ref[:, cols])


def _dot(a, b):
    return jnp.dot(a, b, preferred_element_type=F32)


def _dot_nt(a, b):
    return lax.dot_general(a, b, (((1,), (1,)), ((), ())), preferred_element_type=F32)


def _dot_tn(a, b):
    return lax.dot_general(a, b, (((0,), (0,)), ((), ())), preferred_element_type=F32)


def _emb_ln_kernel(x_ref, g_ref, b_ref, o_ref, ob_ref):
    y = _layer_norm(x_ref[...], g_ref[...], b_ref[...])
    o_ref[...] = y
    ob_ref[...] = y.astype(BF16)


def _emb_ln(x, g, b, tm=256):
    n, d = x.shape
    return pl.pallas_call(
        _emb_ln_kernel,
        grid=(n // tm,),
        in_specs=[pl.BlockSpec((tm, d), lambda i: (i, 0)),
                  pl.BlockSpec((1, d), lambda i: (0, 0)),
                  pl.BlockSpec((1, d), lambda i: (0, 0))],
        out_specs=[pl.BlockSpec((tm, d), lambda i: (i, 0)),
                   pl.BlockSpec((tm, d), lambda i: (i, 0))],
        out_shape=[jax.ShapeDtypeStruct((n, d), F32), jax.ShapeDtypeStruct((n, d), BF16)],
        compiler_params=_params(("arbitrary",)),
        name="emb_ln",
    )(x, g.reshape(1, d), b.reshape(1, d))


def _in_proj_kernel(a_ref, w_ref, o_ref):
    o_ref[...] = _dot(a_ref[...], w_ref[...].astype(BF16))


def _in_proj(xb, w, layer, tm=2048, tn=512):
    n, d = xb.shape
    cols = w.shape[2]
    tm = min(tm, n)
    return pl.pallas_call(
        _in_proj_kernel,
        grid=(n // tm, pl.cdiv(cols, tn)),
        in_specs=[pl.BlockSpec((tm, d), lambda i, j: (i, 0), pipeline_mode=pl.Buffered(1)),
                  pl.BlockSpec((None, d, tn), lambda i, j: (layer, 0, j))],
        out_specs=pl.BlockSpec((tm, tn), lambda i, j: (i, j)),
        out_shape=jax.ShapeDtypeStruct((n, cols), F32),
        compiler_params=_params(("arbitrary", "arbitrary")),
        name="in_proj",
    )(xb, w)


def _sgu_kernel(u_ref, v_ref, g_ref, b_ref, w_ref, bs_ref, o_ref, *, tt):
    row = lax.broadcasted_iota(jnp.int32, (SGU_CHUNK, SGU_CHUNK), 0)
    col = lax.broadcasted_iota(jnp.int32, (SGU_CHUNK, SGU_CHUNK), 1)
    causal = row >= col
    for c in range(tt // SGU_CHUNK):
        rows = slice(c * SGU_CHUNK, (c + 1) * SGU_CHUNK)
        u = _gelu(u_ref[rows, :])
        v = _layer_norm(_gelu(v_ref[rows, :]), g_ref[...], b_ref[...]).astype(BF16)
        for hd in range(SGU_HEADS):
            cols = slice(hd * LANES, (hd + 1) * LANES)
            w = jnp.where(causal, w_ref[hd], 0.0).astype(BF16)
            mixed = _dot(w, v[:, cols]) + bs_ref[:, hd:hd + 1]
            o_ref[rows, cols] = (u[:, cols] * mixed).astype(o_ref.dtype)


def _sgu(h, ln_g, ln_b, w_s, b_s, tt=256):
    n = h.shape[0]
    return pl.pallas_call(
        functools.partial(_sgu_kernel, tt=tt),
        grid=(n // tt,),
        in_specs=[pl.BlockSpec((tt, GW), lambda i: (i, 0)),
                  pl.BlockSpec((tt, GW), lambda i: (i, 1)),
                  pl.BlockSpec((1, GW), lambda i: (0, 0)),
                  pl.BlockSpec((1, GW), lambda i: (0, 0)),
                  pl.BlockSpec((SGU_HEADS, SGU_CHUNK, SGU_CHUNK), lambda i: (0, 0, 0)),
                  pl.BlockSpec((SGU_CHUNK, SGU_HEADS), lambda i: (0, 0))],
        out_specs=pl.BlockSpec((tt, GW), lambda i: (i, 0)),
        out_shape=jax.ShapeDtypeStruct((n, GW), BF16),
        compiler_params=_params(("arbitrary",)),
        name="sgu_mixer",
    )(h, h, ln_g.reshape(1, GW), ln_b.reshape(1, GW), w_s, b_s.T)


def _conv_kernel(a_ref, gate_ref, dww_ref, dwb_ref, ng_ref, nb_ref, pww_ref, pwb_ref, o_ref,
                 hbuf, act, *, tt):
    @pl.when(pl.program_id(1) == 0)
    def _():
        hbuf[:, 0:CONV_HALO, :] = jnp.zeros((GW // LANES, CONV_HALO, LANES), F32)

    first = CONV_HALO - (CONV_WIDTH - 1)
    for c in range(GW // LANES):
        cols = slice(c * LANES, (c + 1) * LANES)
        hbuf[c, CONV_HALO:CONV_HALO + tt, :] = a_ref[:, cols] * _sigmoid(gate_ref[:, cols])
        for r0 in range(0, tt, CONV_ROWS):
            acc = jnp.broadcast_to(dwb_ref[:, cols], (CONV_ROWS, LANES))
            for k in range(CONV_WIDTH):
                acc = acc + dww_ref[k:k + 1, cols] * hbuf[c, r0 + first + k:r0 + first + k + CONV_ROWS, :]
            mu = jnp.mean(acc, axis=-1, keepdims=True)
            xc = acc - mu
            var = jnp.mean(xc * xc, axis=-1, keepdims=True)
            hn = xc * lax.rsqrt(var + LN_EPS) * ng_ref[:, cols] + nb_ref[:, cols]
            act[r0:r0 + CONV_ROWS, cols] = (hn * _sigmoid(hn)).astype(BF16)
        hbuf[c, 0:CONV_HALO, :] = hbuf[c, tt:tt + CONV_HALO, :]
    o_ref[...] = (_dot(act[...], pww_ref[...]) + pwb_ref[...]).astype(o_ref.dtype)


def _conv(h, bsz, dw_w, dw_b, norm_g, norm_b, pw_w, pw_b, tt=512):
    n = h.shape[0]
    nt = n // bsz // tt
    row = lambda b, t: b * nt + t
    vec = lambda: pl.BlockSpec((1, GW), lambda b, t: (0, 0))
    return pl.pallas_call(
        functools.partial(_conv_kernel, tt=tt),
        grid=(bsz, nt),
        in_specs=[pl.BlockSpec((tt, GW), lambda b, t: (row(b, t), 2)),
                  pl.BlockSpec((tt, GW), lambda b, t: (row(b, t), 3)),
                  pl.BlockSpec((CONV_WIDTH, GW), lambda b, t: (0, 0)),
                  vec(), vec(), vec(),
                  pl.BlockSpec((GW, GW), lambda b, t: (0, 0)),
                  vec()],
        out_specs=pl.BlockSpec((tt, GW), lambda b, t: (row(b, t), 0)),
        out_shape=jax.ShapeDtypeStruct((n, GW), BF16),
        scratch_shapes=[pltpu.VMEM((GW // LANES, CONV_HALO + tt, LANES), F32), pltpu.VMEM((tt, GW), BF16)],
        compiler_params=_params(("arbitrary", "arbitrary")),
        name="conv_mixer",
    )(h, h, dw_w, dw_b.reshape(1, GW), norm_g.reshape(1, GW), norm_b.reshape(1, GW),
      pw_w, pw_b.reshape(1, GW))


def _hgrn_constants():
    t = np.arange(HGRN_CHUNK)[:, None]
    j = np.arange(HGRN_CHUNK)[None, :]
    tril = (t >= j).astype(np.float32)
    causal = t >= j
    levels = [causal & ((t ^ j) >= m) & ((t ^ j) < 2 * m) for m in HGRN_LEVELS]
    shifts = [causal & ((t // HGRN_SUB) == (j // HGRN_SUB)) & (t - j == s) for s in range(HGRN_SUB)]
    which = np.full((HGRN_CHUNK, HGRN_CHUNK), -1, np.int32)
    for i, region in enumerate(levels + shifts):
        which[region] = i
    return jnp.asarray(tril, BF16), jnp.asarray(which)


def _hgrn_gates(zq, zf, lb, tril, qbuf, bbuf, kbuf):
    c = HGRN_CHUNK
    f = lb + (1.0 - lb) * _sigmoid(zf)
    lf = jnp.log(f) * (1.0 / math.log(2.0))
    hi = lf.astype(BF16)
    r1 = lf - hi.astype(F32)
    mid = r1.astype(BF16)
    lo = (r1 - mid.astype(F32)).astype(BF16)
    bbuf[HGRN_SUB:HGRN_SUB + c, :] = _dot(tril, hi) + _dot(tril, mid) + _dot(tril, lo)
    kbuf[HGRN_SUB:HGRN_SUB + c, :] = 1.0 - f
    qbuf[...] = zq * _sigmoid(zq)


def _hgrn_terms(ones, qbuf, bbuf, kbuf, terms):
    c = HGRN_CHUNK
    q = qbuf[...]
    b = bbuf[HGRN_SUB:HGRN_SUB + c, :]
    k = kbuf[HGRN_SUB:HGRN_SUB + c, :]

    for i, m in enumerate(HGRN_LEVELS):
        refs = [jnp.broadcast_to(bbuf[HGRN_SUB + g * 2 * m + m - 1:HGRN_SUB + g * 2 * m + m, :], (2 * m, LANES))
                for g in range(c // (2 * m))]
        ref = refs[0] if len(refs) == 1 else jnp.concatenate(refs, axis=0)
        e = jnp.exp2(-jnp.abs(b - ref))
        terms[i * c:(i + 1) * c, :] = _dot_nt((q * e).astype(BF16), (k * e).astype(BF16))

    shifted = []
    for s in range(HGRN_SUB):
        bs = bbuf[HGRN_SUB - s:HGRN_SUB - s + c, :]
        ks = kbuf[HGRN_SUB - s:HGRN_SUB - s + c, :]
        shifted.append((q * ks * jnp.exp2(jnp.minimum(b - bs, 0.0))).astype(BF16))
    terms[len(HGRN_LEVELS) * c:, :] = _dot(jnp.concatenate(shifted, axis=0), ones)

    b_end = b[c - 1:c, :]
    return (q * jnp.exp2(b)).astype(BF16), (k * jnp.exp2(b_end - b)).astype(BF16), jnp.exp2(b_end)


def _hgrn_intra(which_ref, terms, v):
    c = HGRN_CHUNK
    which = which_ref[...]
    scores = jnp.zeros((c, c), F32)
    for i in range(len(HGRN_LEVELS) + HGRN_SUB):
        scores = jnp.where(which == i, terms[i * c:(i + 1) * c, :], scores)
    return _dot(scores.astype(BF16), v)


def _hgrn_kernel(zq_ref, zf_ref, zi_ref, zg_ref, lb_ref, ng_ref, tril_ref, which_ref, o_ref,
                 state, qbuf, bbuf, kbuf, terms, *, tt):
    chunks = tt // HGRN_CHUNK

    @pl.when(pl.program_id(2) == 0)
    def _():
        state[...] = jnp.zeros_like(state)
        bbuf[:, 0:HGRN_SUB, :] = jnp.zeros((bbuf.shape[0], HGRN_SUB, LANES), F32)
        kbuf[:, 0:HGRN_SUB, :] = jnp.zeros((kbuf.shape[0], HGRN_SUB, LANES), F32)

    def window(ci, hh):
        return slice(ci * HGRN_CHUNK, (ci + 1) * HGRN_CHUNK), slice(hh * LANES, (hh + 1) * LANES)

    ones = jnp.ones((LANES, LANES), BF16)
    order = [(ci, hh) for ci in range(chunks) for hh in range(HGRN_STEP_HEADS)]
    for slot, (ci, hh) in enumerate(order):
        rows, cols = window(ci, hh)
        _hgrn_gates(zq_ref[rows, cols], zf_ref[rows, cols], lb_ref[0, :, cols], tril_ref[...],
                    qbuf.at[slot], bbuf.at[slot], kbuf.at[slot])

    def terms_of(slot):
        return _hgrn_terms(ones, qbuf.at[slot], bbuf.at[slot], kbuf.at[slot], terms.at[slot % 2])

    pending = terms_of(0)
    for slot, (ci, hh) in enumerate(order):
        q_dec, k_dec, decay = pending
        if slot + 1 < len(order):
            pending = terms_of(slot + 1)
        rows, cols = window(ci, hh)
        v = zi_ref[rows, cols].astype(BF16)
        st = state[hh]
        o = _dot_nt(q_dec, st.astype(BF16)) + _hgrn_intra(which_ref, terms.at[slot % 2], v)
        state[hh] = decay * st + _dot_tn(v, k_dec)
        o = o * lax.rsqrt(jnp.mean(o * o, axis=-1, keepdims=True) + LN_EPS)
        o_ref[rows, cols] = (o * ng_ref[0, :, cols] * _sigmoid(zg_ref[rows, cols])).astype(o_ref.dtype)


def _hgrn(h, bsz, lower_bound, norm_g, tt=512):
    n = h.shape[0]
    nt = n // bsz // tt
    tril, which = _hgrn_constants()
    width = HGRN_STEP_HEADS * LANES
    slots = HGRN_STEP_HEADS * (tt // HGRN_CHUNK)
    groups = GW // width
    base = 4 * GW // width

    def zspec(part):
        return pl.BlockSpec((tt, width), lambda b, hg, t: (b * nt + t, base + part * groups + hg))

    vec = lambda: pl.BlockSpec((1, 1, width), lambda b, hg, t: (hg, 0, 0))
    return pl.pallas_call(
        functools.partial(_hgrn_kernel, tt=tt),
        grid=(bsz, groups, nt),
        in_specs=[zspec(0), zspec(1), zspec(2), zspec(3), vec(), vec(),
                  pl.BlockSpec((HGRN_CHUNK, HGRN_CHUNK), lambda b, hg, t: (0, 0)),
                  pl.BlockSpec((HGRN_CHUNK, HGRN_CHUNK), lambda b, hg, t: (0, 0))],
        out_specs=pl.BlockSpec((tt, width), lambda b, hg, t: (b * nt + t, hg)),
        out_shape=jax.ShapeDtypeStruct((n, GW), BF16),
        scratch_shapes=[pltpu.VMEM((HGRN_STEP_HEADS, LANES, LANES), F32),
                        pltpu.VMEM((slots, HGRN_CHUNK, LANES), F32),
                        pltpu.VMEM((slots, HGRN_SUB + HGRN_CHUNK, LANES), F32),
                        pltpu.VMEM((slots, HGRN_SUB + HGRN_CHUNK, LANES), F32),
                        pltpu.VMEM((2, (len(HGRN_LEVELS) + HGRN_SUB) * HGRN_CHUNK, LANES), F32)],
        compiler_params=_params(("arbitrary", "arbitrary", "arbitrary")),
        name="hgrn_mixer",
    )(h, h, h, h, lower_bound.reshape(groups, 1, width), norm_g.reshape(groups, 1, width), tril, which)


def _rope_constants():
    lane = np.arange(LANES) % ATTN_HEAD_DIM
    half = ROPE_DIM // 2
    inv_freq = ROPE_THETA ** (-jnp.arange(0, ROPE_DIM, 2, dtype=F32) / ROPE_DIM)
    freq = jnp.where(lane < ROPE_DIM, inv_freq[lane % half], 0.0).astype(F32)
    rot = (lane < ROPE_DIM).astype(np.float32)
    up = ((lane >= half) & (lane < ROPE_DIM)).astype(np.float32)
    dn = (lane < half).astype(np.float32)
    src = np.arange(LANES)[:, None]
    dst = np.arange(LANES)[None, :]
    perm = ((up[None, :] > 0) & (src == dst - half)) | ((dn[None, :] > 0) & (src == dst + half))
    consts = jnp.stack([freq, jnp.asarray(rot), jnp.asarray(up), jnp.asarray(dn)])[:, None, :]
    return consts, jnp.asarray(perm.astype(np.float32), BF16)


def _swa_kernel(sink_ref, pos_ref, q_ref, k_ref, v_ref, rc_ref, perm_ref, o_ref, kprev, vprev):
    t = pl.program_id(1)
    w = WINDOW

    @pl.when(t == 0)
    def _():
        kprev[...] = jnp.zeros_like(kprev)
        vprev[...] = jnp.zeros_like(vprev)

    ang = pos_ref[...].astype(F32) * rc_ref[0]
    c_mul = jnp.cos(ang) * rc_ref[1] + (1.0 - rc_ref[1])
    s_mul = jnp.sin(ang) * (rc_ref[2] - rc_ref[3])
    tiles = GW // LANES

    x = jnp.concatenate([q_ref[:, c * LANES:(c + 1) * LANES] for c in range(tiles)] + [k_ref[...]], axis=0)
    hi = x.astype(BF16)
    lo = (x - hi.astype(F32)).astype(BF16)
    partner = _dot(hi, perm_ref[...]) + _dot(lo, perm_ref[...])
    roped = (x.reshape(tiles + 1, w, LANES) * c_mul + partner.reshape(tiles + 1, w, LANES) * s_mul)

    lane = lax.broadcasted_iota(jnp.int32, (w, LANES), 1)
    low = lane < ATTN_HEAD_DIM
    k_cur = roped[tiles]
    v_cur = v_ref[...]
    k_sw = pltpu.roll(k_cur, ATTN_HEAD_DIM, 1)
    v_sw = pltpu.roll(v_cur, ATTN_HEAD_DIM, 1)
    k_dup = [jnp.where(low, k_cur, k_sw).astype(BF16), jnp.where(low, k_sw, k_cur).astype(BF16)]
    v_dup = [jnp.where(low, v_cur, v_sw).astype(BF16), jnp.where(low, v_sw, v_cur).astype(BF16)]

    qi = lax.broadcasted_iota(jnp.int32, (w, w), 0)
    kj = lax.broadcasted_iota(jnp.int32, (w, w), 1)
    from_prev = kj > qi
    no_prev = jnp.where(t > 0, 0.0, -jnp.inf)
    scale = ATTN_HEAD_DIM ** -0.5
    group = ATTN_Q_HEADS // ATTN_KV_HEADS
    ones = jnp.ones((w, LANES), BF16)

    for kv in range(ATTN_KV_HEADS):
        qs = []
        for pair in range(kv * group // 2, (kv + 1) * group // 2):
            qt = roped[pair] * scale
            qs += [jnp.where(low, qt, 0.0).astype(BF16), jnp.where(low, 0.0, qt).astype(BF16)]
        qst = jnp.concatenate(qs, axis=0)
        s_prev = _dot_nt(qst, kprev[kv]).reshape(group, w, w)
        s_cur = _dot_nt(qst, k_dup[kv]).reshape(group, w, w)
        s = jnp.where(from_prev, s_prev + no_prev, s_cur)
        sink = sink_ref[kv * group:(kv + 1) * group]
        m = jnp.maximum(jnp.max(s, axis=-1, keepdims=True), sink)
        p = jnp.exp(s - m)
        p_prev = jnp.where(from_prev, p, 0.0).astype(BF16).reshape(group * w, w)
        p_cur = jnp.where(from_prev, 0.0, p).astype(BF16).reshape(group * w, w)
        total = (_dot(p_prev, ones) + _dot(p_cur, ones)).reshape(group, w, LANES)
        acc = (_dot(p_prev, vprev[kv]) + _dot(p_cur, v_dup[kv])).reshape(group, w, LANES)
        acc = acc / (total + jnp.exp(sink - m))
        for g in range(0, group, 2):
            pair = (kv * group + g) // 2
            o_ref[:, pair * LANES:(pair + 1) * LANES] = jnp.where(low, acc[g], acc[g + 1]).astype(o_ref.dtype)

    for kv in range(ATTN_KV_HEADS):
        kprev[kv] = k_dup[kv]
        vprev[kv] = v_dup[kv]


def _swa(h, bsz, positions, sinks):
    n = h.shape[0]
    nt = n // bsz // WINDOW
    row = lambda b, t: b * nt + t
    qblk = 8
    kblk = (9 * GW) // LANES
    return pl.pallas_call(
        _swa_kernel,
        grid=(bsz, nt),
        in_specs=[pl.BlockSpec((ATTN_Q_HEADS, 1, 1), lambda b, t: (0, 0, 0)),
                  pl.BlockSpec((WINDOW, 1), lambda b, t: (row(b, t), 0)),
                  pl.BlockSpec((WINDOW, GW), lambda b, t: (row(b, t), qblk)),
                  pl.BlockSpec((WINDOW, LANES), lambda b, t: (row(b, t), kblk)),
                  pl.BlockSpec((WINDOW, LANES), lambda b, t: (row(b, t), kblk + 1)),
                  pl.BlockSpec((4, 1, LANES), lambda b, t: (0, 0, 0)),
                  pl.BlockSpec((LANES, LANES), lambda b, t: (0, 0))],
        out_specs=pl.BlockSpec((WINDOW, GW), lambda b, t: (row(b, t), 0)),
        out_shape=jax.ShapeDtypeStruct((n, GW), BF16),
        scratch_shapes=[pltpu.VMEM((ATTN_KV_HEADS, WINDOW, LANES), BF16),
                        pltpu.VMEM((ATTN_KV_HEADS, WINDOW, LANES), BF16)],
        compiler_params=_params(("arbitrary", "arbitrary")),
        name="swa_mixer",
    )(sinks.astype(F32).reshape(ATTN_Q_HEADS, 1, 1), positions.reshape(n, 1), h, h, h, *_rope_constants())


def _out_proj_kernel(*refs):
    w_ref, x_ref, g_ref, b_ref, o_ref, ob_ref = refs[-6:]
    a_refs = refs[:-6]
    tm, d = o_ref.shape
    kp = w_ref.shape[0] // len(a_refs)

    for c in range(0, d, PROJ_COLS):
        cols = slice(c, c + PROJ_COLS)
        acc = ALPHA * x_ref[:, cols]
        for p, a_ref in enumerate(a_refs):
            acc = acc + _dot(a_ref[...], w_ref[p * kp:(p + 1) * kp, cols])
        o_ref[:, cols] = acc

    def slab(r, carry):
        rows = pl.ds(pl.multiple_of(r * PROJ_ROWS, PROJ_ROWS), PROJ_ROWS)

        def store(cols, y):
            o_ref[rows, cols] = y
            ob_ref[rows, cols] = y.astype(BF16)

        _layer_norm_streamed(lambda cols: o_ref[rows, cols], g_ref, b_ref, store, d)
        return carry

    lax.fori_loop(0, tm // PROJ_ROWS, slab, 0)


def _out_proj(parts, w, layer, x, g, b, tm=256):
    n, d = x.shape
    kp = parts[0].shape[1]
    tm = min(tm, n)
    assert w.shape[1] == kp * len(parts) and d % PROJ_COLS == 0 and tm % PROJ_ROWS == 0
    row = lambda: pl.BlockSpec((tm, d), lambda i: (i, 0))
    vec = lambda: pl.BlockSpec((1, d), lambda i: (0, 0))
    return pl.pallas_call(
        _out_proj_kernel,
        grid=(n // tm,),
        in_specs=[pl.BlockSpec((tm, kp), lambda i: (i, 0)) for _ in parts] + [
            pl.BlockSpec((None, w.shape[1], d), lambda i: (layer, 0, 0), pipeline_mode=pl.Buffered(1)),
            row(), vec(), vec()],
        out_specs=[row(), row()],
        out_shape=[jax.ShapeDtypeStruct((n, d), F32), jax.ShapeDtypeStruct((n, d), BF16)],
        compiler_params=_params(("arbitrary",)),
        name="out_proj_ln",
    )(*parts, w, x, g.reshape(1, d), b.reshape(1, d))


def _proj_ln_tall_kernel(a_ref, w_ref, x_hbm, g_ref, b_ref, o_hbm, ob_hbm,
                         acc, x_buf, o_buf, ob_buf, x_sem, o_sem, ob_sem, *, tail_rows):
    i = pl.program_id(0)
    k = pl.program_id(1)
    last = pl.num_programs(1) - 1
    tm, d = acc.shape
    tk = w_ref.shape[0]
    col_chunks = [slice(c, c + PROJ_COLS) for c in range(0, d, PROJ_COLS)]
    slabs = tm // PROJ_ROWS

    @pl.when(k == 0)
    def _():
        for cols in col_chunks:
            acc[:, cols] = _dot(a_ref[...], w_ref[:, cols].astype(BF16))

    @pl.when((k > 0) & (k < last))
    def _():
        for cols in col_chunks:
            acc[:, cols] += _dot(a_ref[...], w_ref[:, cols].astype(BF16))

    @pl.when(k == last)
    def _():
        real = lax.broadcasted_iota(jnp.int32, (tk, PROJ_COLS), 0) < tail_rows
        for cols in col_chunks:
            acc[:, cols] += _dot(a_ref[...], jnp.where(real, w_ref[:, cols], 0.0).astype(BF16))

    @pl.when(k == last)
    def _():
        def hbm_rows(ref, r):
            return ref.at[pl.ds(pl.multiple_of(i * tm + r * PROJ_ROWS, PROJ_ROWS), PROJ_ROWS), :]

        def x_copy(r, slot):
            return pltpu.make_async_copy(hbm_rows(x_hbm, r), x_buf.at[slot], x_sem.at[slot])

        def o_copy(r, slot):
            return pltpu.make_async_copy(o_buf.at[slot], hbm_rows(o_hbm, r), o_sem.at[slot])

        def ob_copy(r, slot):
            return pltpu.make_async_copy(ob_buf.at[slot], hbm_rows(ob_hbm, r), ob_sem.at[slot])

        def slab(r, slot):
            @pl.when(r + 1 < slabs)
            def _():
                x_copy(r + 1, 1 - slot).start()

            x_copy(r, slot).wait()

            @pl.when(r >= 2)
            def _():
                o_copy(r - 2, slot).wait()
                ob_copy(r - 2, slot).wait()

            rows = pl.ds(pl.multiple_of(r * PROJ_ROWS, PROJ_ROWS), PROJ_ROWS)
            o_buf[slot] = ALPHA * x_buf[slot] + acc[rows, :]

            def store(cols, y):
                o_buf[slot, :, cols] = y
                ob_buf[slot, :, cols] = y.astype(BF16)

            _layer_norm_streamed(lambda cols: o_buf[slot, :, cols], g_ref, b_ref, store, d)
            o_copy(r, slot).start()
            ob_copy(r, slot).start()

        x_copy(0, 0).start()

        def pair(p, carry):
            slab(2 * p, 0)
            slab(2 * p + 1, 1)
            return carry

        lax.fori_loop(0, slabs // 2, pair, 0)
        for r in (slabs - 2, slabs - 1):
            o_copy(r, r % 2).wait()
            ob_copy(r, r % 2).wait()


def _proj_ln_tall(a, w, layer, x, g, b, tm=1024, tk=512):
    n, d = x.shape
    kdim = a.shape[1]
    tm = min(tm, n)
    nk = kdim // tk
    tail_rows = w.shape[1] - (nk - 1) * tk
    assert kdim % tk == 0 and nk >= 2 and 0 < tail_rows <= tk
    assert d % PROJ_COLS == 0 and tm % (2 * PROJ_ROWS) == 0
    vec = lambda: pl.BlockSpec((1, d), lambda i, k: (0, 0))
    hbm = lambda: pl.BlockSpec(memory_space=pl.ANY)
    return pl.pallas_call(
        functools.partial(_proj_ln_tall_kernel, tail_rows=tail_rows),
        grid=(n // tm, nk),
        in_specs=[pl.BlockSpec((tm, tk), lambda i, k: (i, k)),
                  pl.BlockSpec((None, tk, d), lambda i, k: (layer, k, 0)),
                  hbm(), vec(), vec()],
        out_specs=[hbm(), hbm()],
        out_shape=[jax.ShapeDtypeStruct((n, d), F32), jax.ShapeDtypeStruct((n, d), BF16)],
        scratch_shapes=[pltpu.VMEM((tm, d), F32),
                        pltpu.VMEM((2, PROJ_ROWS, d), F32),
                        pltpu.VMEM((2, PROJ_ROWS, d), F32),
                        pltpu.VMEM((2, PROJ_ROWS, d), BF16),
                        pltpu.SemaphoreType.DMA((2,)),
                        pltpu.SemaphoreType.DMA((2,)),
                        pltpu.SemaphoreType.DMA((2,))],
        compiler_params=_params(("arbitrary", "arbitrary")),
        name="ffn_down_ln",
    )(a, w, x, g.reshape(1, d), b.reshape(1, d))


def _ffn_up_kernel(x_ref, wg_ref, wu_ref, o_ref):
    last = pl.num_programs(1) - 1

    @pl.when(pl.program_id(1) < last)
    def _():
        x = x_ref[...]
        gate = _dot(x, wg_ref[...].astype(BF16))
        up = _dot(x, wu_ref[...].astype(BF16))
        o_ref[...] = (gate * _sigmoid(gate) * up).astype(o_ref.dtype)

    @pl.when(pl.program_id(1) == last)
    def _():
        o_ref[...] = jnp.zeros_like(o_ref)


def _ffn_up(xb, wg, wu, layer, tm=2048, tn=256):
    n, d = xb.shape
    f = wg.shape[2]
    assert f % tn == 0 and FFN_PAD - f == tn
    tm = min(tm, n)
    nj = f // tn
    wspec = lambda: pl.BlockSpec((None, d, tn), lambda i, j: (layer, 0, jnp.minimum(j, nj - 1)))
    return pl.pallas_call(
        _ffn_up_kernel,
        grid=(n // tm, nj + 1),
        in_specs=[pl.BlockSpec((tm, d), lambda i, j: (i, 0)), wspec(), wspec()],
        out_specs=pl.BlockSpec((tm, tn), lambda i, j: (i, j)),
        out_shape=jax.ShapeDtypeStruct((n, FFN_PAD), BF16),
        compiler_params=_params(("arbitrary", "arbitrary")),
        name="ffn_up",
    )(xb, wg, wu)


def kernel(x, positions, emb_ln_g, emb_ln_b, w_in, sgu_ln_g, sgu_ln_b, sgu_w, sgu_b, conv_dw_w, conv_dw_b, conv_norm_g, conv_norm_b, conv_pw_w, conv_pw_b, hgrn_lower_bound, hgrn_norm_g, attn_sinks, w_o, ln1_g, ln1_b, w_gate, w_up, w_down, ln2_g, ln2_b):
    bsz, seq, d = x.shape
    n = bsz * seq
    depth = w_in.shape[0]
    p_lb = jax.nn.softmax(hgrn_lower_bound.astype(F32), axis=0)
    lower_bounds = jnp.cumsum(p_lb, axis=0) - p_lb[0]
    wo_b = w_o.astype(BF16)
    pw_b = conv_pw_w.astype(BF16)

    xf, xb = _emb_ln(x.reshape(n, d), emb_ln_g, emb_ln_b)
    for l in range(depth):
        h = _in_proj(xb, w_in, l)
        y_a = _sgu(h, sgu_ln_g[l], sgu_ln_b[l], sgu_w[l], sgu_b[l])
        y_b = _conv(h, bsz, conv_dw_w[l], conv_dw_b[l], conv_norm_g[l], conv_norm_b[l], pw_b[l], conv_pw_b[l])
        y_c = _hgrn(h, bsz, lower_bounds[l], hgrn_norm_g[l])
        y_d = _swa(h, bsz, positions, attn_sinks[l])
        xf, xb = _out_proj((y_a, y_b, y_c, y_d), wo_b, l, xf, ln1_g[l], ln1_b[l])
        hid = _ffn_up(xb, w_gate, w_up, l)
        xf, xb = _proj_ln_tall(hid, w_down, l, xf, ln2_g[l], ln2_b[l])
    return xf.reshape(bsz, seq, d)
```

```python
import functools
import math

import numpy as np
import jax
import jax.numpy as jnp
from jax import lax
from jax.experimental import pallas as pl
from jax.experimental.pallas import tpu as pltpu

F32 = jnp.float32
BF16 = jnp.bfloat16

D_MODEL = 4096
DEPTH = 4
GW = D_MODEL // 4
LANES = 128
SGU_CHUNK = 128
SGU_HEADS = 8
CONV_WIDTH = 31
CONV_HALO = 32
CONV_ROWS = 128
HGRN_CHUNK = 128
HGRN_SUB = 8
HGRN_LEVELS = (64, 32, 16, 8)
HGRN_STEP_HEADS = 2
ATTN_HEAD_DIM = 64
ATTN_Q_HEADS = 16
ATTN_KV_HEADS = 2
WINDOW = 128
ROPE_DIM = 16
ROPE_THETA = 500000.0
FFN_PAD = 11264
FFN_UP_SLABS = 2
PROJ_COLS = 512
PROJ_ROWS = 64
ALPHA = (2 * DEPTH) ** 0.25
LN_EPS = 1e-5
VMEM_LIMIT = 60 * 1024 * 1024


def _params(sem):
    return pltpu.CompilerParams(dimension_semantics=sem, vmem_limit_bytes=VMEM_LIMIT)


def _sigmoid(x):
    return 1.0 / (1.0 + jnp.exp(-x))


def _gelu(x):
    return 0.5 * x * (1.0 + lax.erf(x * (1.0 / math.sqrt(2.0))))


def _layer_norm(x, g, b):
    mu = jnp.mean(x, axis=-1, keepdims=True)
    xc = x - mu
    var = jnp.mean(xc * xc, axis=-1, keepdims=True)
    return xc * lax.rsqrt(var + LN_EPS) * g + b


def _layer_norm_streamed(load, g_ref, b_ref, store, d):
    tiles = [slice(c, c + LANES) for c in range(0, d, LANES)]
    total = load(tiles[0])
    for cols in tiles[1:]:
        total = total + load(cols)
    mu = jnp.sum(total, axis=-1, keepdims=True) / d
    sq = jnp.square(load(tiles[0]) - mu)
    for cols in tiles[1:]:
        sq = sq + jnp.square(load(cols) - mu)
    rs = lax.rsqrt(jnp.sum(sq, axis=-1, keepdims=True) / d + LN_EPS)
    for cols in tiles:
        store(cols, (load(cols) - mu) * rs * g_ref[:, cols] + b_ref[:, cols])


def _dot(a, b):
    return jnp.dot(a, b, preferred_element_type=F32)


def _dot_nt(a, b):
    return lax.dot_general(a, b, (((1,), (1,)), ((), ())), preferred_element_type=F32)


def _dot_tn(a, b):
    return lax.dot_general(a, b, (((0,), (0,)), ((), ())), preferred_element_type=F32)


def _emb_ln_kernel(x_ref, g_ref, b_ref, o_ref, ob_ref):
    y = _layer_norm(x_ref[...], g_ref[...], b_ref[...])
    o_ref[...] = y
    ob_ref[...] = y.astype(BF16)


def _emb_ln(x, g, b, tm=256):
    n, d = x.shape
    return pl.pallas_call(
        _emb_ln_kernel,
        grid=(n // tm,),
        in_specs=[pl.BlockSpec((tm, d), lambda i: (i, 0)),
                  pl.BlockSpec((1, d), lambda i: (0, 0)),
                  pl.BlockSpec((1, d), lambda i: (0, 0))],
        out_specs=[pl.BlockSpec((tm, d), lambda i: (i, 0)),
                   pl.BlockSpec((tm, d), lambda i: (i, 0))],
        out_shape=[jax.ShapeDtypeStruct((n, d), F32), jax.ShapeDtypeStruct((n, d), BF16)],
        compiler_params=_params(("arbitrary",)),
        name="emb_ln",
    )(x, g.reshape(1, d), b.reshape(1, d))


def _in_proj_kernel(a_ref, w_ref, o_ref):
    o_ref[...] = _dot(a_ref[...], w_ref[...].astype(BF16))


def _in_proj(xb, w, layer, tm=2048, tn=512):
    n, d = xb.shape
    cols = w.shape[2]
    tm = min(tm, n)
    return pl.pallas_call(
        _in_proj_kernel,
        grid=(n // tm, pl.cdiv(cols, tn)),
        in_specs=[pl.BlockSpec((tm, d), lambda i, j: (i, 0), pipeline_mode=pl.Buffered(1)),
                  pl.BlockSpec((None, d, tn), lambda i, j: (layer, 0, j))],
        out_specs=pl.BlockSpec((tm, tn), lambda i, j: (i, j)),
        out_shape=jax.ShapeDtypeStruct((n, cols), F32),
        compiler_params=_params(("arbitrary", "arbitrary")),
        name="in_proj",
    )(xb, w)


def _sgu_kernel(u_ref, v_ref, g_ref, b_ref, w_ref, bs_ref, o_ref, *, tt):
    row = lax.broadcasted_iota(jnp.int32, (SGU_CHUNK, SGU_CHUNK), 0)
    col = lax.broadcasted_iota(jnp.int32, (SGU_CHUNK, SGU_CHUNK), 1)
    causal = row >= col
    for c in range(tt // SGU_CHUNK):
        rows = slice(c * SGU_CHUNK, (c + 1) * SGU_CHUNK)
        u = _gelu(u_ref[rows, :])
        v = _layer_norm(_gelu(v_ref[rows, :]), g_ref[...], b_ref[...]).astype(BF16)
        for hd in range(SGU_HEADS):
            cols = slice(hd * LANES, (hd + 1) * LANES)
            w = jnp.where(causal, w_ref[hd], 0.0).astype(BF16)
            mixed = _dot(w, v[:, cols]) + bs_ref[:, hd:hd + 1]
            o_ref[rows, cols] = (u[:, cols] * mixed).astype(o_ref.dtype)


def _sgu(h, ln_g, ln_b, w_s, b_s, tt=256):
    n = h.shape[0]
    return pl.pallas_call(
        functools.partial(_sgu_kernel, tt=tt),
        grid=(n // tt,),
        in_specs=[pl.BlockSpec((tt, GW), lambda i: (i, 0)),
                  pl.BlockSpec((tt, GW), lambda i: (i, 1)),
                  pl.BlockSpec((1, GW), lambda i: (0, 0)),
                  pl.BlockSpec((1, GW), lambda i: (0, 0)),
                  pl.BlockSpec((SGU_HEADS, SGU_CHUNK, SGU_CHUNK), lambda i: (0, 0, 0)),
                  pl.BlockSpec((SGU_CHUNK, SGU_HEADS), lambda i: (0, 0))],
        out_specs=pl.BlockSpec((tt, GW), lambda i: (i, 0)),
        out_shape=jax.ShapeDtypeStruct((n, GW), BF16),
        compiler_params=_params(("arbitrary",)),
        name="sgu_mixer",
    )(h, h, ln_g.reshape(1, GW), ln_b.reshape(1, GW), w_s, b_s.T)


def _conv_kernel(a_ref, gate_ref, dww_ref, dwb_ref, ng_ref, nb_ref, pww_ref, pwb_ref, o_ref,
                 hbuf, act, *, tt):
    @pl.when(pl.program_id(1) == 0)
    def _():
        hbuf[:, 0:CONV_HALO, :] = jnp.zeros((GW // LANES, CONV_HALO, LANES), F32)

    first = CONV_HALO - (CONV_WIDTH - 1)
    for c in range(GW // LANES):
        cols = slice(c * LANES, (c + 1) * LANES)
        hbuf[c, CONV_HALO:CONV_HALO + tt, :] = a_ref[:, cols] * _sigmoid(gate_ref[:, cols])
        for r0 in range(0, tt, CONV_ROWS):
            acc = jnp.broadcast_to(dwb_ref[:, cols], (CONV_ROWS, LANES))
            for k in range(CONV_WIDTH):
                acc = acc + dww_ref[k:k + 1, cols] * hbuf[c, r0 + first + k:r0 + first + k + CONV_ROWS, :]
            mu = jnp.mean(acc, axis=-1, keepdims=True)
            xc = acc - mu
            var = jnp.mean(xc * xc, axis=-1, keepdims=True)
            hn = xc * lax.rsqrt(var + LN_EPS) * ng_ref[:, cols] + nb_ref[:, cols]
            act[r0:r0 + CONV_ROWS, cols] = (hn * _sigmoid(hn)).astype(BF16)
        hbuf[c, 0:CONV_HALO, :] = hbuf[c, tt:tt + CONV_HALO, :]
    o_ref[...] = (_dot(act[...], pww_ref[...]) + pwb_ref[...]).astype(o_ref.dtype)


def _conv(h, bsz, dw_w, dw_b, norm_g, norm_b, pw_w, pw_b, tt=512):
    n = h.shape[0]
    nt = n // bsz // tt
    row = lambda b, t: b * nt + t
    vec = lambda: pl.BlockSpec((1, GW), lambda b, t: (0, 0))
    return pl.pallas_call(
        functools.partial(_conv_kernel, tt=tt),
        grid=(bsz, nt),
        in_specs=[pl.BlockSpec((tt, GW), lambda b, t: (row(b, t), 2)),
                  pl.BlockSpec((tt, GW), lambda b, t: (row(b, t), 3)),
                  pl.BlockSpec((CONV_WIDTH, GW), lambda b, t: (0, 0)),
                  vec(), vec(), vec(),
                  pl.BlockSpec((GW, GW), lambda b, t: (0, 0)),
                  vec()],
        out_specs=pl.BlockSpec((tt, GW), lambda b, t: (row(b, t), 0)),
        out_shape=jax.ShapeDtypeStruct((n, GW), BF16),
        scratch_shapes=[pltpu.VMEM((GW // LANES, CONV_HALO + tt, LANES), F32), pltpu.VMEM((tt, GW), BF16)],
        compiler_params=_params(("arbitrary", "arbitrary")),
        name="conv_mixer",
    )(h, h, dw_w, dw_b.reshape(1, GW), norm_g.reshape(1, GW), norm_b.reshape(1, GW),
      pw_w, pw_b.reshape(1, GW))


def _hgrn_constants():
    t = np.arange(HGRN_CHUNK)[:, None]
    j = np.arange(HGRN_CHUNK)[None, :]
    tril = (t >= j).astype(np.float32)
    causal = t >= j
    levels = [causal & ((t ^ j) >= m) & ((t ^ j) < 2 * m) for m in HGRN_LEVELS]
    shifts = [causal & ((t // HGRN_SUB) == (j // HGRN_SUB)) & (t - j == s) for s in range(HGRN_SUB)]
    which = np.full((HGRN_CHUNK, HGRN_CHUNK), -1, np.int32)
    for i, region in enumerate(levels + shifts):
        which[region] = i
    return jnp.asarray(tril, BF16), jnp.asarray(which)


def _hgrn_gates(zq, zf, lb, tril, qbuf, bbuf, kbuf):
    c = HGRN_CHUNK
    f = lb + (1.0 - lb) * _sigmoid(zf)
    lf = jnp.log(f) * (1.0 / math.log(2.0))
    hi = lf.astype(BF16)
    r1 = lf - hi.astype(F32)
    mid = r1.astype(BF16)
    lo = (r1 - mid.astype(F32)).astype(BF16)
    bbuf[HGRN_SUB:HGRN_SUB + c, :] = _dot(tril, hi) + _dot(tril, mid) + _dot(tril, lo)
    kbuf[HGRN_SUB:HGRN_SUB + c, :] = 1.0 - f
    qbuf[...] = zq * _sigmoid(zq)


def _hgrn_terms(ones, qbuf, bbuf, kbuf, terms):
    c = HGRN_CHUNK
    q = qbuf[...]
    b = bbuf[HGRN_SUB:HGRN_SUB + c, :]
    k = kbuf[HGRN_SUB:HGRN_SUB + c, :]

    for i, m in enumerate(HGRN_LEVELS):
        refs = [jnp.broadcast_to(bbuf[HGRN_SUB + g * 2 * m + m - 1:HGRN_SUB + g * 2 * m + m, :], (2 * m, LANES))
                for g in range(c // (2 * m))]
        ref = refs[0] if len(refs) == 1 else jnp.concatenate(refs, axis=0)
        e = jnp.exp2(-jnp.abs(b - ref))
        terms[i * c:(i + 1) * c, :] = _dot_nt((q * e).astype(BF16), (k * e).astype(BF16))

    shifted = []
    for s in range(HGRN_SUB):
        bs = bbuf[HGRN_SUB - s:HGRN_SUB - s + c, :]
        ks = kbuf[HGRN_SUB - s:HGRN_SUB - s + c, :]
        shifted.append((q * ks * jnp.exp2(jnp.minimum(b - bs, 0.0))).astype(BF16))
    terms[len(HGRN_LEVELS) * c:, :] = _dot(jnp.concatenate(shifted, axis=0), ones)

    b_end = b[c - 1:c, :]
    return (q * jnp.exp2(b)).astype(BF16), (k * jnp.exp2(b_end - b)).astype(BF16), jnp.exp2(b_end)


def _hgrn_intra(which_ref, terms, v):
    c = HGRN_CHUNK
    which = which_ref[...]
    scores = jnp.zeros((c, c), F32)
    for i in range(len(HGRN_LEVELS) + HGRN_SUB):
        scores = jnp.where(which == i, terms[i * c:(i + 1) * c, :], scores)
    return _dot(scores.astype(BF16), v)


def _hgrn_kernel(zq_ref, zf_ref, zi_ref, zg_ref, lb_ref, ng_ref, tril_ref, which_ref, o_ref,
                 state, qbuf, bbuf, kbuf, terms, *, tt):
    chunks = tt // HGRN_CHUNK

    @pl.when(pl.program_id(2) == 0)
    def _():
        state[...] = jnp.zeros_like(state)
        bbuf[:, 0:HGRN_SUB, :] = jnp.zeros((bbuf.shape[0], HGRN_SUB, LANES), F32)
        kbuf[:, 0:HGRN_SUB, :] = jnp.zeros((kbuf.shape[0], HGRN_SUB, LANES), F32)

    def window(ci, hh):
        return slice(ci * HGRN_CHUNK, (ci + 1) * HGRN_CHUNK), slice(hh * LANES, (hh + 1) * LANES)

    ones = jnp.ones((LANES, LANES), BF16)
    order = [(ci, hh) for ci in range(chunks) for hh in range(HGRN_STEP_HEADS)]
    for slot, (ci, hh) in enumerate(order):
        rows, cols = window(ci, hh)
        _hgrn_gates(zq_ref[rows, cols], zf_ref[rows, cols], lb_ref[0, :, cols], tril_ref[...],
                    qbuf.at[slot], bbuf.at[slot], kbuf.at[slot])

    def terms_of(slot):
        return _hgrn_terms(ones, qbuf.at[slot], bbuf.at[slot], kbuf.at[slot], terms.at[slot % 2])

    pending = terms_of(0)
    for slot, (ci, hh) in enumerate(order):
        q_dec, k_dec, decay = pending
        if slot + 1 < len(order):
            pending = terms_of(slot + 1)
        rows, cols = window(ci, hh)
        v = zi_ref[rows, cols].astype(BF16)
        st = state[hh]
        o = _dot_nt(q_dec, st.astype(BF16)) + _hgrn_intra(which_ref, terms.at[slot % 2], v)
        state[hh] = decay * st + _dot_tn(v, k_dec)
        o = o * lax.rsqrt(jnp.mean(o * o, axis=-1, keepdims=True) + LN_EPS)
        o_ref[rows, cols] = (o * ng_ref[0, :, cols] * _sigmoid(zg_ref[rows, cols])).astype(o_ref.dtype)


def _hgrn(h, bsz, lower_bound, norm_g, tt=512):
    n = h.shape[0]
    nt = n // bsz // tt
    tril, which = _hgrn_constants()
    width = HGRN_STEP_HEADS * LANES
    slots = HGRN_STEP_HEADS * (tt // HGRN_CHUNK)
    groups = GW // width
    base = 4 * GW // width

    def zspec(part):
        return pl.BlockSpec((tt, width), lambda b, hg, t: (b * nt + t, base + part * groups + hg))

    vec = lambda: pl.BlockSpec((1, 1, width), lambda b, hg, t: (hg, 0, 0))
    return pl.pallas_call(
        functools.partial(_hgrn_kernel, tt=tt),
        grid=(bsz, groups, nt),
        in_specs=[zspec(0), zspec(1), zspec(2), zspec(3), vec(), vec(),
                  pl.BlockSpec((HGRN_CHUNK, HGRN_CHUNK), lambda b, hg, t: (0, 0)),
                  pl.BlockSpec((HGRN_CHUNK, HGRN_CHUNK), lambda b, hg, t: (0, 0))],
        out_specs=pl.BlockSpec((tt, width), lambda b, hg, t: (b * nt + t, hg)),
        out_shape=jax.ShapeDtypeStruct((n, GW), BF16),
        scratch_shapes=[pltpu.VMEM((HGRN_STEP_HEADS, LANES, LANES), F32),
                        pltpu.VMEM((slots, HGRN_CHUNK, LANES), F32),
                        pltpu.VMEM((slots, HGRN_SUB + HGRN_CHUNK, LANES), F32),
                        pltpu.VMEM((slots, HGRN_SUB + HGRN_CHUNK, LANES), F32),
                        pltpu.VMEM((2, (len(HGRN_LEVELS) + HGRN_SUB) * HGRN_CHUNK, LANES), F32)],
        compiler_params=_params(("arbitrary", "arbitrary", "arbitrary")),
        name="hgrn_mixer",
    )(h, h, h, h, lower_bound.reshape(groups, 1, width), norm_g.reshape(groups, 1, width), tril, which)


def _rope_constants():
    lane = np.arange(LANES) % ATTN_HEAD_DIM
    half = ROPE_DIM // 2
    inv_freq = ROPE_THETA ** (-jnp.arange(0, ROPE_DIM, 2, dtype=F32) / ROPE_DIM)
    freq = jnp.where(lane < ROPE_DIM, inv_freq[lane % half], 0.0).astype(F32)
    rot = (lane < ROPE_DIM).astype(np.float32)
    up = ((lane >= half) & (lane < ROPE_DIM)).astype(np.float32)
    dn = (lane < half).astype(np.float32)
    src = np.arange(LANES)[:, None]
    dst = np.arange(LANES)[None, :]
    perm = ((up[None, :] > 0) & (src == dst - half)) | ((dn[None, :] > 0) & (src == dst + half))
    consts = jnp.stack([freq, jnp.asarray(rot), jnp.asarray(up), jnp.asarray(dn)])[:, None, :]
    return consts, jnp.asarray(perm.astype(np.float32), BF16)


def _swa_kernel(sink_ref, pos_ref, q_ref, k_ref, v_ref, rc_ref, perm_ref, o_ref, kprev, vprev):
    t = pl.program_id(1)
    w = WINDOW

    @pl.when(t == 0)
    def _():
        kprev[...] = jnp.zeros_like(kprev)
        vprev[...] = jnp.zeros_like(vprev)

    ang = pos_ref[...].astype(F32) * rc_ref[0]
    c_mul = jnp.cos(ang) * rc_ref[1] + (1.0 - rc_ref[1])
    s_mul = jnp.sin(ang) * (rc_ref[2] - rc_ref[3])
    tiles = GW // LANES

    x = jnp.concatenate([q_ref[:, c * LANES:(c + 1) * LANES] for c in range(tiles)] + [k_ref[...]], axis=0)
    hi = x.astype(BF16)
    lo = (x - hi.astype(F32)).astype(BF16)
    partner = _dot(hi, perm_ref[...]) + _dot(lo, perm_ref[...])
    roped = (x.reshape(tiles + 1, w, LANES) * c_mul + partner.reshape(tiles + 1, w, LANES) * s_mul)

    lane = lax.broadcasted_iota(jnp.int32, (w, LANES), 1)
    low = lane < ATTN_HEAD_DIM
    k_cur = roped[tiles]
    v_cur = v_ref[...]
    k_sw = pltpu.roll(k_cur, ATTN_HEAD_DIM, 1)
    v_sw = pltpu.roll(v_cur, ATTN_HEAD_DIM, 1)
    k_dup = [jnp.where(low, k_cur, k_sw).astype(BF16), jnp.where(low, k_sw, k_cur).astype(BF16)]
    v_dup = [jnp.where(low, v_cur, v_sw).astype(BF16), jnp.where(low, v_sw, v_cur).astype(BF16)]

    qi = lax.broadcasted_iota(jnp.int32, (w, w), 0)
    kj = lax.broadcasted_iota(jnp.int32, (w, w), 1)
    from_prev = kj > qi
    no_prev = jnp.where(t > 0, 0.0, -jnp.inf)
    scale = ATTN_HEAD_DIM ** -0.5
    group = ATTN_Q_HEADS // ATTN_KV_HEADS
    ones = jnp.ones((w, LANES), BF16)

    for kv in range(ATTN_KV_HEADS):
        qs = []
        for pair in range(kv * group // 2, (kv + 1) * group // 2):
            qt = roped[pair] * scale
            qs += [jnp.where(low, qt, 0.0).astype(BF16), jnp.where(low, 0.0, qt).astype(BF16)]
        qst = jnp.concatenate(qs, axis=0)
        s_prev = _dot_nt(qst, kprev[kv]).reshape(group, w, w)
        s_cur = _dot_nt(qst, k_dup[kv]).reshape(group, w, w)
        s = jnp.where(from_prev, s_prev + no_prev, s_cur)
        sink = sink_ref[kv * group:(kv + 1) * group]
        m = jnp.maximum(jnp.max(s, axis=-1, keepdims=True), sink)
        p = jnp.exp(s - m)
        p_prev = jnp.where(from_prev, p, 0.0).astype(BF16).reshape(group * w, w)
        p_cur = jnp.where(from_prev, 0.0, p).astype(BF16).reshape(group * w, w)
        total = (_dot(p_prev, ones) + _dot(p_cur, ones)).reshape(group, w, LANES)
        acc = (_dot(p_prev, vprev[kv]) + _dot(p_cur, v_dup[kv])).reshape(group, w, LANES)
        acc = acc / (total + jnp.exp(sink - m))
        for g in range(0, group, 2):
            pair = (kv * group + g) // 2
            o_ref[:, pair * LANES:(pair + 1) * LANES] = jnp.where(low, acc[g], acc[g + 1]).astype(o_ref.dtype)

    for kv in range(ATTN_KV_HEADS):
        kprev[kv] = k_dup[kv]
        vprev[kv] = v_dup[kv]


def _swa(h, bsz, positions, sinks):
    n = h.shape[0]
    nt = n // bsz // WINDOW
    row = lambda b, t: b * nt + t
    qblk = 8
    kblk = (9 * GW) // LANES
    return pl.pallas_call(
        _swa_kernel,
        grid=(bsz, nt),
        in_specs=[pl.BlockSpec((ATTN_Q_HEADS, 1, 1), lambda b, t: (0, 0, 0)),
                  pl.BlockSpec((WINDOW, 1), lambda b, t: (row(b, t), 0)),
                  pl.BlockSpec((WINDOW, GW), lambda b, t: (row(b, t), qblk)),
                  pl.BlockSpec((WINDOW, LANES), lambda b, t: (row(b, t), kblk)),
                  pl.BlockSpec((WINDOW, LANES), lambda b, t: (row(b, t), kblk + 1)),
                  pl.BlockSpec((4, 1, LANES), lambda b, t: (0, 0, 0)),
                  pl.BlockSpec((LANES, LANES), lambda b, t: (0, 0))],
        out_specs=pl.BlockSpec((WINDOW, GW), lambda b, t: (row(b, t), 0)),
        out_shape=jax.ShapeDtypeStruct((n, GW), BF16),
        scratch_shapes=[pltpu.VMEM((ATTN_KV_HEADS, WINDOW, LANES), BF16),
                        pltpu.VMEM((ATTN_KV_HEADS, WINDOW, LANES), BF16)],
        compiler_params=_params(("arbitrary", "arbitrary")),
        name="swa_mixer",
    )(sinks.astype(F32).reshape(ATTN_Q_HEADS, 1, 1), positions.reshape(n, 1), h, h, h, *_rope_constants())


def _out_proj_kernel(*refs):
    w_ref, x_ref, g_ref, b_ref, o_ref, ob_ref = refs[-6:]
    a_refs = refs[:-6]
    tm, d = o_ref.shape
    kp = w_ref.shape[0] // len(a_refs)

    for c in range(0, d, PROJ_COLS):
        cols = slice(c, c + PROJ_COLS)
        acc = ALPHA * x_ref[:, cols]
        for p, a_ref in enumerate(a_refs):
            acc = acc + _dot(a_ref[...], w_ref[p * kp:(p + 1) * kp, cols])
        o_ref[:, cols] = acc

    def slab(r, carry):
        rows = pl.ds(pl.multiple_of(r * PROJ_ROWS, PROJ_ROWS), PROJ_ROWS)

        def store(cols, y):
            o_ref[rows, cols] = y
            ob_ref[rows, cols] = y.astype(BF16)

        _layer_norm_streamed(lambda cols: o_ref[rows, cols], g_ref, b_ref, store, d)
        return carry

    lax.fori_loop(0, tm // PROJ_ROWS, slab, 0)


def _out_proj(parts, w, layer, x, g, b, tm=256):
    n, d = x.shape
    kp = parts[0].shape[1]
    tm = min(tm, n)
    assert w.shape[1] == kp * len(parts) and d % PROJ_COLS == 0 and tm % PROJ_ROWS == 0
    row = lambda: pl.BlockSpec((tm, d), lambda i: (i, 0))
    vec = lambda: pl.BlockSpec((1, d), lambda i: (0, 0))
    return pl.pallas_call(
        _out_proj_kernel,
        grid=(n // tm,),
        in_specs=[pl.BlockSpec((tm, kp), lambda i: (i, 0)) for _ in parts] + [
            pl.BlockSpec((None, w.shape[1], d), lambda i: (layer, 0, 0), pipeline_mode=pl.Buffered(1)),
            row(), vec(), vec()],
        out_specs=[row(), row()],
        out_shape=[jax.ShapeDtypeStruct((n, d), F32), jax.ShapeDtypeStruct((n, d), BF16)],
        compiler_params=_params(("arbitrary",)),
        name="out_proj_ln",
    )(*parts, w, x, g.reshape(1, d), b.reshape(1, d))


def _proj_ln_tall_kernel(a_ref, w_ref, x_hbm, g_ref, b_ref, o_hbm, ob_hbm,
                         acc, x_buf, o_buf, ob_buf, x_sem, o_sem, ob_sem, *, tail_rows):
    i = pl.program_id(0)
    k = pl.program_id(1)
    last = pl.num_programs(1) - 1
    tm, d = acc.shape
    tk = w_ref.shape[0]
    col_chunks = [slice(c, c + PROJ_COLS) for c in range(0, d, PROJ_COLS)]
    slabs = tm // PROJ_ROWS

    @pl.when(k == 0)
    def _():
        for cols in col_chunks:
            acc[:, cols] = _dot(a_ref[...], w_ref[:, cols].astype(BF16))

    @pl.when((k > 0) & (k < last))
    def _():
        for cols in col_chunks:
            acc[:, cols] += _dot(a_ref[...], w_ref[:, cols].astype(BF16))

    @pl.when(k == last)
    def _():
        real = lax.broadcasted_iota(jnp.int32, (tk, PROJ_COLS), 0) < tail_rows
        for cols in col_chunks:
            acc[:, cols] += _dot(a_ref[...], jnp.where(real, w_ref[:, cols], 0.0).astype(BF16))

    @pl.when(k == last)
    def _():
        def hbm_rows(ref, r):
            return ref.at[pl.ds(pl.multiple_of(i * tm + r * PROJ_ROWS, PROJ_ROWS), PROJ_ROWS), :]

        def x_copy(r, slot):
            return pltpu.make_async_copy(hbm_rows(x_hbm, r), x_buf.at[slot], x_sem.at[slot])

        def o_copy(r, slot):
            return pltpu.make_async_copy(o_buf.at[slot], hbm_rows(o_hbm, r), o_sem.at[slot])

        def ob_copy(r, slot):
            return pltpu.make_async_copy(ob_buf.at[slot], hbm_rows(ob_hbm, r), ob_sem.at[slot])

        def slab(r, slot):
            @pl.when(r + 1 < slabs)
            def _():
                x_copy(r + 1, 1 - slot).start()

            x_copy(r, slot).wait()

            @pl.when(r >= 2)
            def _():
                o_copy(r - 2, slot).wait()
                ob_copy(r - 2, slot).wait()

            rows = pl.ds(pl.multiple_of(r * PROJ_ROWS, PROJ_ROWS), PROJ_ROWS)
            o_buf[slot] = ALPHA * x_buf[slot] + acc[rows, :]

            def store(cols, y):
                o_buf[slot, :, cols] = y
                ob_buf[slot, :, cols] = y.astype(BF16)

            _layer_norm_streamed(lambda cols: o_buf[slot, :, cols], g_ref, b_ref, store, d)
            o_copy(r, slot).start()
            ob_copy(r, slot).start()

        x_copy(0, 0).start()

        def pair(p, carry):
            slab(2 * p, 0)
            slab(2 * p + 1, 1)
            return carry

        lax.fori_loop(0, slabs // 2, pair, 0)
        for r in (slabs - 2, slabs - 1):
            o_copy(r, r % 2).wait()
            ob_copy(r, r % 2).wait()


def _proj_ln_tall(a, w, layer, x, g, b, tm=1024, tk=512):
    n, d = x.shape
    kdim = a.shape[1]
    tm = min(tm, n)
    nk = kdim // tk
    tail_rows = w.shape[1] - (nk - 1) * tk
    assert kdim % tk == 0 and nk >= 2 and 0 < tail_rows <= tk
    assert d % PROJ_COLS == 0 and tm % (2 * PROJ_ROWS) == 0
    vec = lambda: pl.BlockSpec((1, d), lambda i, k: (0, 0))
    hbm = lambda: pl.BlockSpec(memory_space=pl.ANY)
    return pl.pallas_call(
        functools.partial(_proj_ln_tall_kernel, tail_rows=tail_rows),
        grid=(n // tm, nk),
        in_specs=[pl.BlockSpec((tm, tk), lambda i, k: (i, k)),
                  pl.BlockSpec((None, tk, d), lambda i, k: (layer, k, 0)),
                  hbm(), vec(), vec()],
        out_specs=[hbm(), hbm()],
        out_shape=[jax.ShapeDtypeStruct((n, d), F32), jax.ShapeDtypeStruct((n, d), BF16)],
        scratch_shapes=[pltpu.VMEM((tm, d), F32),
                        pltpu.VMEM((2, PROJ_ROWS, d), F32),
                        pltpu.VMEM((2, PROJ_ROWS, d), F32),
                        pltpu.VMEM((2, PROJ_ROWS, d), BF16),
                        pltpu.SemaphoreType.DMA((2,)),
                        pltpu.SemaphoreType.DMA((2,)),
                        pltpu.SemaphoreType.DMA((2,))],
        compiler_params=_params(("arbitrary", "arbitrary")),
        name="ffn_down_ln",
    )(a, w, x, g.reshape(1, d), b.reshape(1, d))


def _ffn_up_kernel(x_ref, wg_ref, wu_ref, o_ref):
    last = pl.num_programs(1) - 1

    @pl.when(pl.program_id(1) < last)
    def _():
        wg = wg_ref[...].astype(BF16)
        wu = wu_ref[...].astype(BF16)
        slab = x_ref.shape[0] // FFN_UP_SLABS
        for r in range(FFN_UP_SLABS):
            rows = slice(r * slab, (r + 1) * slab)
            gate = _dot(x_ref[rows, :], wg)
            up = _dot(x_ref[rows, :], wu)
            o_ref[rows, :] = (gate * _sigmoid(gate) * up).astype(o_ref.dtype)

    @pl.when(pl.program_id(1) == last)
    def _():
        o_ref[...] = jnp.zeros_like(o_ref)


def _ffn_up(xb, wg, wu, layer, tm=2048, tn=256):
    n, d = xb.shape
    f = wg.shape[2]
    assert f % tn == 0 and FFN_PAD - f == tn
    tm = min(tm, n)
    nj = f // tn
    wspec = lambda: pl.BlockSpec((None, d, tn), lambda i, j: (layer, 0, jnp.minimum(j, nj - 1)))
    return pl.pallas_call(
        _ffn_up_kernel,
        grid=(n // tm, nj + 1),
        in_specs=[pl.BlockSpec((tm, d), lambda i, j: (i, 0)), wspec(), wspec()],
        out_specs=pl.BlockSpec((tm, tn), lambda i, j: (i, j)),
        out_shape=jax.ShapeDtypeStruct((n, FFN_PAD), BF16),
        compiler_params=_params(("arbitrary", "arbitrary")),
        name="ffn_up",
    )(xb, wg, wu)


def kernel(x, positions, emb_ln_g, emb_ln_b, w_in, sgu_ln_g, sgu_ln_b, sgu_w, sgu_b, conv_dw_w, conv_dw_b, conv_norm_g, conv_norm_b, conv_pw_w, conv_pw_b, hgrn_lower_bound, hgrn_norm_g, attn_sinks, w_o, ln1_g, ln1_b, w_gate, w_up, w_down, ln2_g, ln2_b):
    bsz, seq, d = x.shape
    n = bsz * seq
    depth = w_in.shape[0]
    p_lb = jax.nn.softmax(hgrn_lower_bound.astype(F32), axis=0)
    lower_bounds = jnp.cumsum(p_lb, axis=0) - p_lb[0]
    wo_b = w_o.astype(BF16)
    pw_b = conv_pw_w.astype(BF16)

    xf, xb = _emb_ln(x.reshape(n, d), emb_ln_g, emb_ln_b)
    for l in range(depth):
        h = _in_proj(xb, w_in, l)
        y_a = _sgu(h, sgu_ln_g[l], sgu_ln_b[l], sgu_w[l], sgu_b[l])
        y_b = _conv(h, bsz, conv_dw_w[l], conv_dw_b[l], conv_norm_g[l], conv_norm_b[l], pw_b[l], conv_pw_b[l])
        y_c = _hgrn(h, bsz, lower_bounds[l], hgrn_norm_g[l])
        y_d = _swa(h, bsz, positions, attn_sinks[l])
        xf, xb = _out_proj((y_a, y_b, y_c, y_d), wo_b, l, xf, ln1_g[l], ln1_b[l])
        hid = _ffn_up(xb, w_gate, w_up, l)
        xf, xb = _proj_ln_tall(hid, w_down, l, xf, ln2_g[l], ln2_b[l])
    return xf.reshape(bsz, seq, d)
```

```python
import functools
import math

import numpy as np
import jax
import jax.numpy as jnp
from jax import lax
from jax.experimental import pallas as pl
from jax.experimental.pallas import tpu as pltpu

F32 = jnp.float32
BF16 = jnp.bfloat16

D_MODEL = 4096
DEPTH = 4
GW = D_MODEL // 4
LANES = 128
SGU_CHUNK = 128
SGU_HEADS = 8
CONV_WIDTH = 31
CONV_HALO = 32
CONV_ROWS = 128
HGRN_CHUNK = 128
HGRN_SUB = 8
HGRN_LEVELS = (64, 32, 16, 8)
HGRN_STEP_HEADS = 2
ATTN_HEAD_DIM = 64
ATTN_Q_HEADS = 16
ATTN_KV_HEADS = 2
WINDOW = 128
ROPE_DIM = 16
ROPE_THETA = 500000.0
FFN_PAD = 11264
FFN_UP_SLABS = 2
PROJ_COLS = 512
PROJ_ROWS = 64
ALPHA = (2 * DEPTH) ** 0.25
LN_EPS = 1e-5
VMEM_LIMIT = 60 * 1024 * 1024


def _params(sem):
    return pltpu.CompilerParams(dimension_semantics=sem, vmem_limit_bytes=VMEM_LIMIT)


def _sigmoid(x):
    return 1.0 / (1.0 + jnp.exp(-x))


def _gelu(x):
    return 0.5 * x * (1.0 + lax.erf(x * (1.0 / math.sqrt(2.0))))


def _layer_norm(x, g, b):
    mu = jnp.mean(x, axis=-1, keepdims=True)
    xc = x - mu
    var = jnp.mean(xc * xc, axis=-1, keepdims=True)
    return xc * lax.rsqrt(var + LN_EPS) * g + b


def _layer_norm_streamed(load, g_ref, b_ref, store, d):
    tiles = [slice(c, c + LANES) for c in range(0, d, LANES)]
    total = load(tiles[0])
    for cols in tiles[1:]:
        total = total + load(cols)
    mu = jnp.sum(total, axis=-1, keepdims=True) / d
    sq = jnp.square(load(tiles[0]) - mu)
    for cols in tiles[1:]:
        sq = sq + jnp.square(load(cols) - mu)
    rs = lax.rsqrt(jnp.sum(sq, axis=-1, keepdims=True) / d + LN_EPS)
    for cols in tiles:
        store(cols, (load(cols) - mu) * rs * g_ref[:, cols] + b_ref[:, cols])


def _dot(a, b):
    return jnp.dot(a, b, preferred_element_type=F32)


def _dot_nt(a, b):
    return lax.dot_general(a, b, (((1,), (1,)), ((), ())), preferred_element_type=F32)


def _dot_tn(a, b):
    return lax.dot_general(a, b, (((0,), (0,)), ((), ())), preferred_element_type=F32)


def _emb_ln_kernel(x_ref, g_ref, b_ref, o_ref, ob_ref):
    y = _layer_norm(x_ref[...], g_ref[...], b_ref[...])
    o_ref[...] = y
    ob_ref[...] = y.astype(BF16)


def _emb_ln(x, g, b, tm=256):
    n, d = x.shape
    return pl.pallas_call(
        _emb_ln_kernel,
        grid=(n // tm,),
        in_specs=[pl.BlockSpec((tm, d), lambda i: (i, 0)),
                  pl.BlockSpec((1, d), lambda i: (0, 0)),
                  pl.BlockSpec((1, d), lambda i: (0, 0))],
        out_specs=[pl.BlockSpec((tm, d), lambda i: (i, 0)),
                   pl.BlockSpec((tm, d), lambda i: (i, 0))],
        out_shape=[jax.ShapeDtypeStruct((n, d), F32), jax.ShapeDtypeStruct((n, d), BF16)],
        compiler_params=_params(("arbitrary",)),
        name="emb_ln",
    )(x, g.reshape(1, d), b.reshape(1, d))


def _in_proj_kernel(a_ref, w_ref, o_ref):
    o_ref[...] = _dot(a_ref[...], w_ref[...].astype(BF16))


def _in_proj(xb, w, layer, tm=2048, tn=512):
    n, d = xb.shape
    cols = w.shape[2]
    tm = min(tm, n)
    return pl.pallas_call(
        _in_proj_kernel,
        grid=(n // tm, pl.cdiv(cols, tn)),
        in_specs=[pl.BlockSpec((tm, d), lambda i, j: (i, 0), pipeline_mode=pl.Buffered(1)),
                  pl.BlockSpec((None, d, tn), lambda i, j: (layer, 0, j))],
        out_specs=pl.BlockSpec((tm, tn), lambda i, j: (i, j)),
        out_shape=jax.ShapeDtypeStruct((n, cols), F32),
        compiler_params=_params(("arbitrary", "arbitrary")),
        name="in_proj",
    )(xb, w)


def _sgu_kernel(u_ref, v_ref, g_ref, b_ref, w_ref, bs_ref, o_ref, *, tt):
    row = lax.broadcasted_iota(jnp.int32, (SGU_CHUNK, SGU_CHUNK), 0)
    col = lax.broadcasted_iota(jnp.int32, (SGU_CHUNK, SGU_CHUNK), 1)
    causal = row >= col
    for c in range(tt // SGU_CHUNK):
        rows = slice(c * SGU_CHUNK, (c + 1) * SGU_CHUNK)
        u = _gelu(u_ref[rows, :])
        v = _layer_norm(_gelu(v_ref[rows, :]), g_ref[...], b_ref[...]).astype(BF16)
        for hd in range(SGU_HEADS):
            cols = slice(hd * LANES, (hd + 1) * LANES)
            w = jnp.where(causal, w_ref[hd], 0.0).astype(BF16)
            mixed = _dot(w, v[:, cols]) + bs_ref[:, hd:hd + 1]
            o_ref[rows, cols] = (u[:, cols] * mixed).astype(o_ref.dtype)


def _sgu(h, ln_g, ln_b, w_s, b_s, tt=256):
    n = h.shape[0]
    return pl.pallas_call(
        functools.partial(_sgu_kernel, tt=tt),
        grid=(n // tt,),
        in_specs=[pl.BlockSpec((tt, GW), lambda i: (i, 0)),
                  pl.BlockSpec((tt, GW), lambda i: (i, 1)),
                  pl.BlockSpec((1, GW), lambda i: (0, 0)),
                  pl.BlockSpec((1, GW), lambda i: (0, 0)),
                  pl.BlockSpec((SGU_HEADS, SGU_CHUNK, SGU_CHUNK), lambda i: (0, 0, 0)),
                  pl.BlockSpec((SGU_CHUNK, SGU_HEADS), lambda i: (0, 0))],
        out_specs=pl.BlockSpec((tt, GW), lambda i: (i, 0)),
        out_shape=jax.ShapeDtypeStruct((n, GW), BF16),
        compiler_params=_params(("arbitrary",)),
        name="sgu_mixer",
    )(h, h, ln_g.reshape(1, GW), ln_b.reshape(1, GW), w_s, b_s.T)


def _conv_kernel(a_ref, gate_ref, dww_ref, dwb_ref, ng_ref, nb_ref, pww_ref, pwb_ref, o_ref,
                 hbuf, act, *, tt):
    @pl.when(pl.program_id(1) == 0)
    def _():
        hbuf[:, 0:CONV_HALO, :] = jnp.zeros((GW // LANES, CONV_HALO, LANES), F32)

    first = CONV_HALO - (CONV_WIDTH - 1)
    for c in range(GW // LANES):
        cols = slice(c * LANES, (c + 1) * LANES)
        hbuf[c, CONV_HALO:CONV_HALO + tt, :] = a_ref[:, cols] * _sigmoid(gate_ref[:, cols])
        for r0 in range(0, tt, CONV_ROWS):
            acc = jnp.broadcast_to(dwb_ref[:, cols], (CONV_ROWS, LANES))
            for k in range(CONV_WIDTH):
                acc = acc + dww_ref[k:k + 1, cols] * hbuf[c, r0 + first + k:r0 + first + k + CONV_ROWS, :]
            mu = jnp.mean(acc, axis=-1, keepdims=True)
            xc = acc - mu
            var = jnp.mean(xc * xc, axis=-1, keepdims=True)
            hn = xc * lax.rsqrt(var + LN_EPS) * ng_ref[:, cols] + nb_ref[:, cols]
            act[r0:r0 + CONV_ROWS, cols] = (hn * _sigmoid(hn)).astype(BF16)
        hbuf[c, 0:CONV_HALO, :] = hbuf[c, tt:tt + CONV_HALO, :]
    o_ref[...] = (_dot(act[...], pww_ref[...]) + pwb_ref[...]).astype(o_ref.dtype)


def _conv(h, bsz, dw_w, dw_b, norm_g, norm_b, pw_w, pw_b, tt=512):
    n = h.shape[0]
    nt = n // bsz // tt
    row = lambda b, t: b * nt + t
    vec = lambda: pl.BlockSpec((1, GW), lambda b, t: (0, 0))
    return pl.pallas_call(
        functools.partial(_conv_kernel, tt=tt),
        grid=(bsz, nt),
        in_specs=[pl.BlockSpec((tt, GW), lambda b, t: (row(b, t), 2)),
                  pl.BlockSpec((tt, GW), lambda b, t: (row(b, t), 3)),
                  pl.BlockSpec((CONV_WIDTH, GW), lambda b, t: (0, 0)),
                  vec(), vec(), vec(),
                  pl.BlockSpec((GW, GW), lambda b, t: (0, 0)),
                  vec()],
        out_specs=pl.BlockSpec((tt, GW), lambda b, t: (row(b, t), 0)),
        out_shape=jax.ShapeDtypeStruct((n, GW), BF16),
        scratch_shapes=[pltpu.VMEM((GW // LANES, CONV_HALO + tt, LANES), F32), pltpu.VMEM((tt, GW), BF16)],
        compiler_params=_params(("arbitrary", "arbitrary")),
        name="conv_mixer",
    )(h, h, dw_w, dw_b.reshape(1, GW), norm_g.reshape(1, GW), norm_b.reshape(1, GW),
      pw_w, pw_b.reshape(1, GW))


def _hgrn_constants():
    t = np.arange(HGRN_CHUNK)[:, None]
    j = np.arange(HGRN_CHUNK)[None, :]
    tril = (t >= j).astype(np.float32)
    causal = t >= j
    levels = [causal & ((t ^ j) >= m) & ((t ^ j) < 2 * m) for m in HGRN_LEVELS]
    shifts = [causal & ((t // HGRN_SUB) == (j // HGRN_SUB)) & (t - j == s) for s in range(HGRN_SUB)]
    which = np.full((HGRN_CHUNK, HGRN_CHUNK), -1, np.int32)
    for i, region in enumerate(levels + shifts):
        which[region] = i
    return jnp.asarray(tril, BF16), jnp.asarray(which)


def _hgrn_gates(zq, zf, lb, tril, qbuf, bbuf, kbuf):
    c = HGRN_CHUNK
    f = lb + (1.0 - lb) * _sigmoid(zf)
    lf = jnp.log(f) * (1.0 / math.log(2.0))
    hi = lf.astype(BF16)
    r1 = lf - hi.astype(F32)
    mid = r1.astype(BF16)
    lo = (r1 - mid.astype(F32)).astype(BF16)
    bbuf[HGRN_SUB:HGRN_SUB + c, :] = _dot(tril, hi) + _dot(tril, mid) + _dot(tril, lo)
    kbuf[HGRN_SUB:HGRN_SUB + c, :] = 1.0 - f
    qbuf[...] = zq * _sigmoid(zq)


def _hgrn_terms(ones, qbuf, bbuf, kbuf, terms):
    c = HGRN_CHUNK
    q = qbuf[...]
    b = bbuf[HGRN_SUB:HGRN_SUB + c, :]
    k = kbuf[HGRN_SUB:HGRN_SUB + c, :]

    for i, m in enumerate(HGRN_LEVELS):
        refs = [jnp.broadcast_to(bbuf[HGRN_SUB + g * 2 * m + m - 1:HGRN_SUB + g * 2 * m + m, :], (2 * m, LANES))
                for g in range(c // (2 * m))]
        ref = refs[0] if len(refs) == 1 else jnp.concatenate(refs, axis=0)
        e = jnp.exp2(-jnp.abs(b - ref))
        terms[i * c:(i + 1) * c, :] = _dot_nt((q * e).astype(BF16), (k * e).astype(BF16))

    shifted = []
    for s in range(HGRN_SUB):
        bs = bbuf[HGRN_SUB - s:HGRN_SUB - s + c, :]
        ks = kbuf[HGRN_SUB - s:HGRN_SUB - s + c, :]
        shifted.append((q * ks * jnp.exp2(jnp.minimum(b - bs, 0.0))).astype(BF16))
    terms[len(HGRN_LEVELS) * c:, :] = _dot(jnp.concatenate(shifted, axis=0), ones)

    b_end = b[c - 1:c, :]
    return (q * jnp.exp2(b)).astype(BF16), (k * jnp.exp2(b_end - b)).astype(BF16), jnp.exp2(b_end)


def _hgrn_intra(which_ref, terms, v):
    c = HGRN_CHUNK
    which = which_ref[...]
    scores = jnp.zeros((c, c), F32)
    for i in range(len(HGRN_LEVELS) + HGRN_SUB):
        scores = jnp.where(which == i, terms[i * c:(i + 1) * c, :], scores)
    return _dot(scores.astype(BF16), v)


def _hgrn_kernel(zq_ref, zf_ref, zi_ref, zg_ref, lb_ref, ng_ref, tril_ref, which_ref, o_ref,
                 state, qbuf, bbuf, kbuf, terms, *, tt):
    chunks = tt // HGRN_CHUNK

    @pl.when(pl.program_id(2) == 0)
    def _():
        state[...] = jnp.zeros_like(state)
        bbuf[:, 0:HGRN_SUB, :] = jnp.zeros((bbuf.shape[0], HGRN_SUB, LANES), F32)
        kbuf[:, 0:HGRN_SUB, :] = jnp.zeros((kbuf.shape[0], HGRN_SUB, LANES), F32)

    def window(ci, hh):
        return slice(ci * HGRN_CHUNK, (ci + 1) * HGRN_CHUNK), slice(hh * LANES, (hh + 1) * LANES)

    ones = jnp.ones((LANES, LANES), BF16)
    order = [(ci, hh) for ci in range(chunks) for hh in range(HGRN_STEP_HEADS)]
    for slot, (ci, hh) in enumerate(order):
        rows, cols = window(ci, hh)
        _hgrn_gates(zq_ref[rows, cols], zf_ref[rows, cols], lb_ref[0, :, cols], tril_ref[...],
                    qbuf.at[slot], bbuf.at[slot], kbuf.at[slot])

    def terms_of(slot):
        return _hgrn_terms(ones, qbuf.at[slot], bbuf.at[slot], kbuf.at[slot], terms.at[slot % 2])

    pending = terms_of(0)
    for slot, (ci, hh) in enumerate(order):
        q_dec, k_dec, decay = pending
        if slot + 1 < len(order):
            pending = terms_of(slot + 1)
        rows, cols = window(ci, hh)
        v = zi_ref[rows, cols].astype(BF16)
        st = state[hh]
        o = _dot_nt(q_dec, st.astype(BF16)) + _hgrn_intra(which_ref, terms.at[slot % 2], v)
        state[hh] = decay * st + _dot_tn(v, k_dec)
        o = o * lax.rsqrt(jnp.mean(o * o, axis=-1, keepdims=True) + LN_EPS)
        o_ref[rows, cols] = (o * ng_ref[0, :, cols] * _sigmoid(zg_ref[rows, cols])).astype(o_ref.dtype)


def _hgrn(h, bsz, lower_bound, norm_g, tt=1024):
    n = h.shape[0]
    nt = n // bsz // tt
    tril, which = _hgrn_constants()
    width = HGRN_STEP_HEADS * LANES
    slots = HGRN_STEP_HEADS * (tt // HGRN_CHUNK)
    groups = GW // width
    base = 4 * GW // width

    def zspec(part):
        return pl.BlockSpec((tt, width), lambda b, hg, t: (b * nt + t, base + part * groups + hg))

    vec = lambda: pl.BlockSpec((1, 1, width), lambda b, hg, t: (hg, 0, 0))
    return pl.pallas_call(
        functools.partial(_hgrn_kernel, tt=tt),
        grid=(bsz, groups, nt),
        in_specs=[zspec(0), zspec(1), zspec(2), zspec(3), vec(), vec(),
                  pl.BlockSpec((HGRN_CHUNK, HGRN_CHUNK), lambda b, hg, t: (0, 0)),
                  pl.BlockSpec((HGRN_CHUNK, HGRN_CHUNK), lambda b, hg, t: (0, 0))],
        out_specs=pl.BlockSpec((tt, width), lambda b, hg, t: (b * nt + t, hg)),
        out_shape=jax.ShapeDtypeStruct((n, GW), BF16),
        scratch_shapes=[pltpu.VMEM((HGRN_STEP_HEADS, LANES, LANES), F32),
                        pltpu.VMEM((slots, HGRN_CHUNK, LANES), F32),
                        pltpu.VMEM((slots, HGRN_SUB + HGRN_CHUNK, LANES), F32),
                        pltpu.VMEM((slots, HGRN_SUB + HGRN_CHUNK, LANES), F32),
                        pltpu.VMEM((2, (len(HGRN_LEVELS) + HGRN_SUB) * HGRN_CHUNK, LANES), F32)],
        compiler_params=_params(("arbitrary", "arbitrary", "arbitrary")),
        name="hgrn_mixer",
    )(h, h, h, h, lower_bound.reshape(groups, 1, width), norm_g.reshape(groups, 1, width), tril, which)


def _rope_constants():
    lane = np.arange(LANES) % ATTN_HEAD_DIM
    half = ROPE_DIM // 2
    inv_freq = ROPE_THETA ** (-jnp.arange(0, ROPE_DIM, 2, dtype=F32) / ROPE_DIM)
    freq = jnp.where(lane < ROPE_DIM, inv_freq[lane % half], 0.0).astype(F32)
    rot = (lane < ROPE_DIM).astype(np.float32)
    up = ((lane >= half) & (lane < ROPE_DIM)).astype(np.float32)
    dn = (lane < half).astype(np.float32)
    src = np.arange(LANES)[:, None]
    dst = np.arange(LANES)[None, :]
    perm = ((up[None, :] > 0) & (src == dst - half)) | ((dn[None, :] > 0) & (src == dst + half))
    consts = jnp.stack([freq, jnp.asarray(rot), jnp.asarray(up), jnp.asarray(dn)])[:, None, :]
    return consts, jnp.asarray(perm.astype(np.float32), BF16)


def _swa_kernel(sink_ref, pos_ref, q_ref, k_ref, v_ref, rc_ref, perm_ref, o_ref, kprev, vprev):
    t = pl.program_id(1)
    w = WINDOW

    @pl.when(t == 0)
    def _():
        kprev[...] = jnp.zeros_like(kprev)
        vprev[...] = jnp.zeros_like(vprev)

    ang = pos_ref[...].astype(F32) * rc_ref[0]
    c_mul = jnp.cos(ang) * rc_ref[1] + (1.0 - rc_ref[1])
    s_mul = jnp.sin(ang) * (rc_ref[2] - rc_ref[3])
    tiles = GW // LANES

    x = jnp.concatenate([q_ref[:, c * LANES:(c + 1) * LANES] for c in range(tiles)] + [k_ref[...]], axis=0)
    hi = x.astype(BF16)
    lo = (x - hi.astype(F32)).astype(BF16)
    partner = _dot(hi, perm_ref[...]) + _dot(lo, perm_ref[...])
    roped = (x.reshape(tiles + 1, w, LANES) * c_mul + partner.reshape(tiles + 1, w, LANES) * s_mul)

    lane = lax.broadcasted_iota(jnp.int32, (w, LANES), 1)
    low = lane < ATTN_HEAD_DIM
    k_cur = roped[tiles]
    v_cur = v_ref[...]
    k_sw = pltpu.roll(k_cur, ATTN_HEAD_DIM, 1)
    v_sw = pltpu.roll(v_cur, ATTN_HEAD_DIM, 1)
    k_dup = [jnp.where(low, k_cur, k_sw).astype(BF16), jnp.where(low, k_sw, k_cur).astype(BF16)]
    v_dup = [jnp.where(low, v_cur, v_sw).astype(BF16), jnp.where(low, v_sw, v_cur).astype(BF16)]

    qi = lax.broadcasted_iota(jnp.int32, (w, w), 0)
    kj = lax.broadcasted_iota(jnp.int32, (w, w), 1)
    from_prev = kj > qi
    no_prev = jnp.where(t > 0, 0.0, -jnp.inf)
    scale = ATTN_HEAD_DIM ** -0.5
    group = ATTN_Q_HEADS // ATTN_KV_HEADS
    ones = jnp.ones((w, LANES), BF16)

    for kv in range(ATTN_KV_HEADS):
        qs = []
        for pair in range(kv * group // 2, (kv + 1) * group // 2):
            qt = roped[pair] * scale
            qs += [jnp.where(low, qt, 0.0).astype(BF16), jnp.where(low, 0.0, qt).astype(BF16)]
        qst = jnp.concatenate(qs, axis=0)
        s_prev = _dot_nt(qst, kprev[kv]).reshape(group, w, w)
        s_cur = _dot_nt(qst, k_dup[kv]).reshape(group, w, w)
        s = jnp.where(from_prev, s_prev + no_prev, s_cur)
        sink = sink_ref[kv * group:(kv + 1) * group]
        m = jnp.maximum(jnp.max(s, axis=-1, keepdims=True), sink)
        p = jnp.exp(s - m)
        p_prev = jnp.where(from_prev, p, 0.0).astype(BF16).reshape(group * w, w)
        p_cur = jnp.where(from_prev, 0.0, p).astype(BF16).reshape(group * w, w)
        total = (_dot(p_prev, ones) + _dot(p_cur, ones)).reshape(group, w, LANES)
        acc = (_dot(p_prev, vprev[kv]) + _dot(p_cur, v_dup[kv])).reshape(group, w, LANES)
        acc = acc / (total + jnp.exp(sink - m))
        for g in range(0, group, 2):
            pair = (kv * group + g) // 2
            o_ref[:, pair * LANES:(pair + 1) * LANES] = jnp.where(low, acc[g], acc[g + 1]).astype(o_ref.dtype)

    for kv in range(ATTN_KV_HEADS):
        kprev[kv] = k_dup[kv]
        vprev[kv] = v_dup[kv]


def _swa(h, bsz, positions, sinks):
    n = h.shape[0]
    nt = n // bsz // WINDOW
    row = lambda b, t: b * nt + t
    qblk = 8
    kblk = (9 * GW) // LANES
    return pl.pallas_call(
        _swa_kernel,
        grid=(bsz, nt),
        in_specs=[pl.BlockSpec((ATTN_Q_HEADS, 1, 1), lambda b, t: (0, 0, 0)),
                  pl.BlockSpec((WINDOW, 1), lambda b, t: (row(b, t), 0)),
                  pl.BlockSpec((WINDOW, GW), lambda b, t: (row(b, t), qblk)),
                  pl.BlockSpec((WINDOW, LANES), lambda b, t: (row(b, t), kblk)),
                  pl.BlockSpec((WINDOW, LANES), lambda b, t: (row(b, t), kblk + 1)),
                  pl.BlockSpec((4, 1, LANES), lambda b, t: (0, 0, 0)),
                  pl.BlockSpec((LANES, LANES), lambda b, t: (0, 0))],
        out_specs=pl.BlockSpec((WINDOW, GW), lambda b, t: (row(b, t), 0)),
        out_shape=jax.ShapeDtypeStruct((n, GW), BF16),
        scratch_shapes=[pltpu.VMEM((ATTN_KV_HEADS, WINDOW, LANES), BF16),
                        pltpu.VMEM((ATTN_KV_HEADS, WINDOW, LANES), BF16)],
        compiler_params=_params(("arbitrary", "arbitrary")),
        name="swa_mixer",
    )(sinks.astype(F32).reshape(ATTN_Q_HEADS, 1, 1), positions.reshape(n, 1), h, h, h, *_rope_constants())


def _out_proj_kernel(*refs):
    w_ref, x_ref, g_ref, b_ref, o_ref, ob_ref = refs[-6:]
    a_refs = refs[:-6]
    tm, d = o_ref.shape
    kp = w_ref.shape[0] // len(a_refs)

    for c in range(0, d, PROJ_COLS):
        cols = slice(c, c + PROJ_COLS)
        acc = ALPHA * x_ref[:, cols]
        for p, a_ref in enumerate(a_refs):
            acc = acc + _dot(a_ref[...], w_ref[p * kp:(p + 1) * kp, cols])
        o_ref[:, cols] = acc

    def slab(r, carry):
        rows = pl.ds(pl.multiple_of(r * PROJ_ROWS, PROJ_ROWS), PROJ_ROWS)

        def store(cols, y):
            o_ref[rows, cols] = y
            ob_ref[rows, cols] = y.astype(BF16)

        _layer_norm_streamed(lambda cols: o_ref[rows, cols], g_ref, b_ref, store, d)
        return carry

    lax.fori_loop(0, tm // PROJ_ROWS, slab, 0)


def _out_proj(parts, w, layer, x, g, b, tm=256):
    n, d = x.shape
    kp = parts[0].shape[1]
    tm = min(tm, n)
    assert w.shape[1] == kp * len(parts) and d % PROJ_COLS == 0 and tm % PROJ_ROWS == 0
    row = lambda: pl.BlockSpec((tm, d), lambda i: (i, 0))
    vec = lambda: pl.BlockSpec((1, d), lambda i: (0, 0))
    return pl.pallas_call(
        _out_proj_kernel,
        grid=(n // tm,),
        in_specs=[pl.BlockSpec((tm, kp), lambda i: (i, 0)) for _ in parts] + [
            pl.BlockSpec((None, w.shape[1], d), lambda i: (layer, 0, 0), pipeline_mode=pl.Buffered(1)),
            row(), vec(), vec()],
        out_specs=[row(), row()],
        out_shape=[jax.ShapeDtypeStruct((n, d), F32), jax.ShapeDtypeStruct((n, d), BF16)],
        compiler_params=_params(("arbitrary",)),
        name="out_proj_ln",
    )(*parts, w, x, g.reshape(1, d), b.reshape(1, d))


def _proj_ln_tall_kernel(a_ref, w_ref, x_hbm, g_ref, b_ref, o_hbm, ob_hbm,
                         acc, x_buf, o_buf, ob_buf, x_sem, o_sem, ob_sem, *, tail_rows):
    i = pl.program_id(0)
    k = pl.program_id(1)
    last = pl.num_programs(1) - 1
    tm, d = acc.shape
    tk = w_ref.shape[0]
    col_chunks = [slice(c, c + PROJ_COLS) for c in range(0, d, PROJ_COLS)]
    slabs = tm // PROJ_ROWS

    @pl.when(k == 0)
    def _():
        for cols in col_chunks:
            acc[:, cols] = _dot(a_ref[...], w_ref[:, cols].astype(BF16))

    @pl.when((k > 0) & (k < last))
    def _():
        for cols in col_chunks:
            acc[:, cols] += _dot(a_ref[...], w_ref[:, cols].astype(BF16))

    @pl.when(k == last)
    def _():
        real = lax.broadcasted_iota(jnp.int32, (tk, PROJ_COLS), 0) < tail_rows
        for cols in col_chunks:
            acc[:, cols] += _dot(a_ref[...], jnp.where(real, w_ref[:, cols], 0.0).astype(BF16))

    @pl.when(k == last)
    def _():
        def hbm_rows(ref, r):
            return ref.at[pl.ds(pl.multiple_of(i * tm + r * PROJ_ROWS, PROJ_ROWS), PROJ_ROWS), :]

        def x_copy(r, slot):
            return pltpu.make_async_copy(hbm_rows(x_hbm, r), x_buf.at[slot], x_sem.at[slot])

        def o_copy(r, slot):
            return pltpu.make_async_copy(o_buf.at[slot], hbm_rows(o_hbm, r), o_sem.at[slot])

        def ob_copy(r, slot):
            return pltpu.make_async_copy(ob_buf.at[slot], hbm_rows(ob_hbm, r), ob_sem.at[slot])

        def slab(r, slot):
            @pl.when(r + 1 < slabs)
            def _():
                x_copy(r + 1, 1 - slot).start()

            x_copy(r, slot).wait()

            @pl.when(r >= 2)
            def _():
                o_copy(r - 2, slot).wait()
                ob_copy(r - 2, slot).wait()

            rows = pl.ds(pl.multiple_of(r * PROJ_ROWS, PROJ_ROWS), PROJ_ROWS)
            o_buf[slot] = ALPHA * x_buf[slot] + acc[rows, :]

            def store(cols, y):
                o_buf[slot, :, cols] = y
                ob_buf[slot, :, cols] = y.astype(BF16)

            _layer_norm_streamed(lambda cols: o_buf[slot, :, cols], g_ref, b_ref, store, d)
            o_copy(r, slot).start()
            ob_copy(r, slot).start()

        x_copy(0, 0).start()

        def pair(p, carry):
            slab(2 * p, 0)
            slab(2 * p + 1, 1)
            return carry

        lax.fori_loop(0, slabs // 2, pair, 0)
        for r in (slabs - 2, slabs - 1):
            o_copy(r, r % 2).wait()
            ob_copy(r, r % 2).wait()


def _proj_ln_tall(a, w, layer, x, g, b, tm=1024, tk=512):
    n, d = x.shape
    kdim = a.shape[1]
    tm = min(tm, n)
    nk = kdim // tk
    tail_rows = w.shape[1] - (nk - 1) * tk
    assert kdim % tk == 0 and nk >= 2 and 0 < tail_rows <= tk
    assert d % PROJ_COLS == 0 and tm % (2 * PROJ_ROWS) == 0
    vec = lambda: pl.BlockSpec((1, d), lambda i, k: (0, 0))
    hbm = lambda: pl.BlockSpec(memory_space=pl.ANY)
    return pl.pallas_call(
        functools.partial(_proj_ln_tall_kernel, tail_rows=tail_rows),
        grid=(n // tm, nk),
        in_specs=[pl.BlockSpec((tm, tk), lambda i, k: (i, k)),
                  pl.BlockSpec((None, tk, d), lambda i, k: (layer, k, 0)),
                  hbm(), vec(), vec()],
        out_specs=[hbm(), hbm()],
        out_shape=[jax.ShapeDtypeStruct((n, d), F32), jax.ShapeDtypeStruct((n, d), BF16)],
        scratch_shapes=[pltpu.VMEM((tm, d), F32),
                        pltpu.VMEM((2, PROJ_ROWS, d), F32),
                        pltpu.VMEM((2, PROJ_ROWS, d), F32),
                        pltpu.VMEM((2, PROJ_ROWS, d), BF16),
                        pltpu.SemaphoreType.DMA((2,)),
                        pltpu.SemaphoreType.DMA((2,)),
                        pltpu.SemaphoreType.DMA((2,))],
        compiler_params=_params(("arbitrary", "arbitrary")),
        name="ffn_down_ln",
    )(a, w, x, g.reshape(1, d), b.reshape(1, d))


def _ffn_up_kernel(x_ref, wg_ref, wu_ref, o_ref):
    last = pl.num_programs(1) - 1

    @pl.when(pl.program_id(1) < last)
    def _():
        wg = wg_ref[...].astype(BF16)
        wu = wu_ref[...].astype(BF16)
        slab = x_ref.shape[0] // FFN_UP_SLABS
        for r in range(FFN_UP_SLABS):
            rows = slice(r * slab, (r + 1) * slab)
            gate = _dot(x_ref[rows, :], wg)
            up = _dot(x_ref[rows, :], wu)
            o_ref[rows, :] = (gate * _sigmoid(gate) * up).astype(o_ref.dtype)

    @pl.when(pl.program_id(1) == last)
    def _():
        o_ref[...] = jnp.zeros_like(o_ref)


def _ffn_up(xb, wg, wu, layer, tm=2048, tn=256):
    n, d = xb.shape
    f = wg.shape[2]
    assert f % tn == 0 and FFN_PAD - f == tn
    tm = min(tm, n)
    nj = f // tn
    wspec = lambda: pl.BlockSpec((None, d, tn), lambda i, j: (layer, 0, jnp.minimum(j, nj - 1)))
    return pl.pallas_call(
        _ffn_up_kernel,
        grid=(n // tm, nj + 1),
        in_specs=[pl.BlockSpec((tm, d), lambda i, j: (i, 0)), wspec(), wspec()],
        out_specs=pl.BlockSpec((tm, tn), lambda i, j: (i, j)),
        out_shape=jax.ShapeDtypeStruct((n, FFN_PAD), BF16),
        compiler_params=_params(("arbitrary", "arbitrary")),
        name="ffn_up",
    )(xb, wg, wu)


def kernel(x, positions, emb_ln_g, emb_ln_b, w_in, sgu_ln_g, sgu_ln_b, sgu_w, sgu_b, conv_dw_w, conv_dw_b, conv_norm_g, conv_norm_b, conv_pw_w, conv_pw_b, hgrn_lower_bound, hgrn_norm_g, attn_sinks, w_o, ln1_g, ln1_b, w_gate, w_up, w_down, ln2_g, ln2_b):
    bsz, seq, d = x.shape
    n = bsz * seq
    depth = w_in.shape[0]
    p_lb = jax.nn.softmax(hgrn_lower_bound.astype(F32), axis=0)
    lower_bounds = jnp.cumsum(p_lb, axis=0) - p_lb[0]
    wo_b = w_o.astype(BF16)
    pw_b = conv_pw_w.astype(BF16)

    xf, xb = _emb_ln(x.reshape(n, d), emb_ln_g, emb_ln_b)
    for l in range(depth):
        h = _in_proj(xb, w_in, l)
        y_a = _sgu(h, sgu_ln_g[l], sgu_ln_b[l], sgu_w[l], sgu_b[l])
        y_b = _conv(h, bsz, conv_dw_w[l], conv_dw_b[l], conv_norm_g[l], conv_norm_b[l], pw_b[l], conv_pw_b[l])
        y_c = _hgrn(h, bsz, lower_bounds[l], hgrn_norm_g[l])
        y_d = _swa(h, bsz, positions, attn_sinks[l])
        xf, xb = _out_proj((y_a, y_b, y_c, y_d), wo_b, l, xf, ln1_g[l], ln1_b[l])
        hid = _ffn_up(xb, w_gate, w_up, l)
        xf, xb = _proj_ln_tall(hid, w_down, l, xf, ln2_g[l], ln2_b[l])
    return xf.reshape(bsz, seq, d)
```
